```python
import jax, jax.numpy as jnp
from jax import lax
import numpy as np

D_MODEL = 4096
BATCH = 1
SEQ = 16384
DEPTH = 1
DEC_BATCH = 2
DEC_SEQ = 4096
PAST_LEN = 128

HEAD_DIM = 128
N_HEADS = D_MODEL // HEAD_DIM
N_NA_HEADS = N_HEADS // 2
N_SW_HEADS = N_HEADS - N_NA_HEADS
N_SW_KV_HEADS = max(1, N_SW_HEADS // 4)
D_NA = N_NA_HEADS * HEAD_DIM
D_SW = N_SW_HEADS * HEAD_DIM
D_SW_KV = N_SW_KV_HEADS * HEAD_DIM
D_IN = 3 * D_NA + D_SW + 2 * D_SW_KV
D_FF = -(-8 * D_MODEL // (3 * 256)) * 256
GRID_W = 64
NA_MAX_ROWS = 8
NA_COLS = 16
NA_QROWS = 2
SW_WINDOW = 128
SW_BLOCK = 128
N_META = 16
EPS = 1e-6
NEG_INF = -1e30

kernel_name = 'hybrid_natten_swa_encoder'


def rmsnorm(x, g):
    x32 = x.astype(jnp.float32)
    y = x32 * lax.rsqrt(jnp.mean(x32 * x32, axis=-1, keepdims=True) + EPS)
    return y.astype(x.dtype) * g


def alibi_slopes(n):
    return jnp.exp2(-8.0 * (jnp.arange(n, dtype=jnp.float32) + 1.0) / n)


def neighbourhood_attention(q, k, v, qm, km, vm, rpb):
    B, T, H, D = q.shape
    M = km.shape[1]
    rows = T // GRID_W
    kh = min(NA_MAX_ROWS, rows)
    nr = min(kh + 1, rows)
    nblk = rows // NA_QROWS
    nq = NA_QROWS * GRID_W
    nk = nr * GRID_W
    scale = D ** -0.5
    rs = jnp.clip(jnp.arange(rows) - kh // 2, 0, rows - kh)
    cs = jnp.clip(jnp.arange(GRID_W) - NA_COLS // 2, 0, GRID_W - NA_COLS)
    r0 = jnp.arange(nblk) * NA_QROWS
    key_rows = jnp.clip(rs[r0], 0, rows - nr)[:, None] + jnp.arange(nr)[None, :]
    kg = k.reshape(B, rows, GRID_W, H, D)[:, key_rows].reshape(B, nblk, nk, H, D)
    vg = v.reshape(B, rows, GRID_W, H, D)[:, key_rows].reshape(B, nblk, nk, H, D)
    qb = (q * scale).reshape(B, nblk, nq, H, D)
    qr = r0[:, None] + jnp.arange(nq)[None, :] // GRID_W
    qc = jnp.arange(nq) % GRID_W
    kr = jnp.repeat(key_rows, GRID_W, axis=1)
    kc = jnp.tile(jnp.arange(GRID_W), nr)
    rq = rs[qr][:, :, None]
    cq = cs[qc][:, None]
    col_ok = (kc[None, :] >= cq) & (kc[None, :] < cq + NA_COLS)
    valid = (kr[:, None, :] >= rq) & (kr[:, None, :] < rq + kh) & col_ok[None]
    ir = jnp.clip(kr[:, None, :] - qr[:, :, None] + (NA_MAX_ROWS - 1), 0, 2 * NA_MAX_ROWS - 2)
    ic = jnp.clip(kc[None, :] - qc[:, None] + (NA_COLS - 1), 0, 2 * NA_COLS - 2)
    ic = jnp.broadcast_to(ic[None], ir.shape)
    bias = rpb[:, ir, ic].astype(jnp.float32)
    s = jnp.einsum('bnqhd,bnkhd->bhnqk', qb, kg).astype(jnp.float32) + bias[None]
    s = jnp.where(valid[None, None], s, NEG_INF)
    sm = jnp.einsum('bnqhd,bmhd->bhnqm', qb, km).astype(jnp.float32)
    p = jax.nn.softmax(jnp.concatenate([sm, s], axis=-1), axis=-1).astype(v.dtype)
    o = (jnp.einsum('bhnqm,bmhd->bnqhd', p[..., :M], vm)
         + jnp.einsum('bhnqk,bnkhd->bnqhd', p[..., M:], vg)).reshape(B, T, H * D)
    pm = jax.nn.softmax(jnp.einsum('bmhd,bnhd->bhmn', qm * scale, km).astype(jnp.float32), axis=-1).astype(v.dtype)
    om = jnp.einsum('bhmn,bnhd->bmhd', pm, vm).reshape(B, M, H * D)
    return om, o


def window_attention(q, k, v, qm, km, vm, sink, slopes):
    B, T, Hq, D = q.shape
    Hkv = k.shape[2]
    G = Hq // Hkv
    M = km.shape[1]
    nb = T // SW_BLOCK
    scale = D ** -0.5
    slope_kg = slopes.reshape(Hkv, G)
    sink_kg = sink.astype(jnp.float32).reshape(Hkv, G)
    slope = slope_kg[:, :, None, None, None]
    qb = (q * scale).reshape(B, nb, SW_BLOCK, Hkv, G, D)
    pad = ((0, 0), (SW_BLOCK, SW_BLOCK), (0, 0), (0, 0))
    kp = jnp.pad(k, pad).reshape(B, nb + 2, SW_BLOCK, Hkv, D)
    vp = jnp.pad(v, pad).reshape(B, nb + 2, SW_BLOCK, Hkv, D)
    kb = jnp.concatenate([kp[:, :-2], kp[:, 1:-1], kp[:, 2:]], axis=2)
    vb = jnp.concatenate([vp[:, :-2], vp[:, 1:-1], vp[:, 2:]], axis=2)
    qpos = jnp.arange(nb)[:, None] * SW_BLOCK + jnp.arange(SW_BLOCK)[None, :]
    kpos = (jnp.arange(nb)[:, None] - 1) * SW_BLOCK + jnp.arange(3 * SW_BLOCK)[None, :]
    dist = jnp.abs(qpos[:, :, None] - kpos[:, None, :])
    valid = (dist <= SW_WINDOW) & (kpos[:, None, :] >= 0) & (kpos[:, None, :] < T)
    s = jnp.einsum('bnqkgd,bnskd->bkgnqs', qb, kb).astype(jnp.float32) - slope * dist.astype(jnp.float32)
    s = jnp.where(valid, s, NEG_INF)
    mdist = jnp.minimum(M + qpos[:, :, None] - jnp.arange(M)[None, None, :], SW_WINDOW).astype(jnp.float32)
    sm = jnp.einsum('bnqkgd,bmkd->bkgnqm', qb, km).astype(jnp.float32) - slope * mdist
    sk = jnp.broadcast_to(sink_kg[None, :, :, None, None, None], sm.shape[:-1] + (1,))
    p = jax.nn.softmax(jnp.concatenate([sm, s, sk], axis=-1), axis=-1).astype(v.dtype)
    o = (jnp.einsum('bkgnqm,bmkd->bnqkgd', p[..., :M], vm)
         + jnp.einsum('bkgnqs,bnskd->bnqkgd', p[..., M:M + 3 * SW_BLOCK], vb)).reshape(B, T, Hq * D)
    qmb = (qm * scale).reshape(B, M, Hkv, G, D)
    km2 = jnp.concatenate([km, k[:, :SW_BLOCK]], axis=1)
    vm2 = jnp.concatenate([vm, v[:, :SW_BLOCK]], axis=1)
    kmpos = jnp.concatenate([jnp.arange(M), M + jnp.arange(SW_BLOCK)])
    mqd = jnp.abs(jnp.arange(M)[:, None] - kmpos[None, :])
    smq = (jnp.einsum('bmkgd,bskd->bkgms', qmb, km2).astype(jnp.float32)
           - slope_kg[:, :, None, None] * mqd.astype(jnp.float32))
    smq = jnp.where(mqd <= SW_WINDOW, smq, NEG_INF)
    skm = jnp.broadcast_to(sink_kg[None, :, :, None, None], smq.shape[:-1] + (1,))
    pmq = jax.nn.softmax(jnp.concatenate([smq, skm], axis=-1), axis=-1).astype(v.dtype)
    om = jnp.einsum('bkgms,bskd->bmkgd', pmq[..., :-1], vm2).reshape(B, M, Hq * D)
    return om, o


def encoder_layer(x, w_in, w_out, rpb, sink, g_na, g_sw, g_pre_mix, g_post_mix,
                  g_pre_ffn, g_post_ffn, w_gate, w_up, w_down, slopes):
    B, L, _ = x.shape
    M = N_META
    h = rmsnorm(x, g_pre_mix)
    proj = h @ w_in
    cuts = [D_NA, 2 * D_NA, 3 * D_NA, 3 * D_NA + D_SW, 3 * D_NA + D_SW + D_SW_KV]
    qa, ka, va, qs, ks, vs = jnp.split(proj, cuts, axis=-1)
    qa = qa.reshape(B, L, N_NA_HEADS, HEAD_DIM)
    ka = ka.reshape(B, L, N_NA_HEADS, HEAD_DIM)
    va = va.reshape(B, L, N_NA_HEADS, HEAD_DIM)
    qs = qs.reshape(B, L, N_SW_HEADS, HEAD_DIM)
    ks = ks.reshape(B, L, N_SW_KV_HEADS, HEAD_DIM)
    vs = vs.reshape(B, L, N_SW_KV_HEADS, HEAD_DIM)
    oa_m, oa = neighbourhood_attention(qa[:, M:], ka[:, M:], va[:, M:], qa[:, :M], ka[:, :M], va[:, :M], rpb)
    os_m, os_r = window_attention(qs[:, M:], ks[:, M:], vs[:, M:], qs[:, :M], ks[:, :M], vs[:, :M], sink, slopes)
    o_na = jnp.concatenate([oa_m, oa], axis=1)
    o_sw = jnp.concatenate([os_m, os_r], axis=1)
    mix = jnp.concatenate([rmsnorm(o_na, g_na), rmsnorm(o_sw, g_sw)], axis=-1) @ w_out
    x = x + rmsnorm(mix, g_post_mix)
    h = rmsnorm(x, g_pre_ffn)
    f = (jax.nn.silu(h @ w_gate) * (h @ w_up)) @ w_down
    return x + rmsnorm(f, g_post_ffn)


def setup_inputs(seed: int = 0) -> dict:
    key = jax.random.key(seed)
    ks = jax.random.split(key, 16)
    f32 = jnp.float32

    def nrm(k, shape, s):
        return jax.random.normal(k, shape, f32) * s

    def gain(k, shape):
        return 1.0 + 0.05 * jax.random.normal(k, shape, f32)

    return {
        'x_prompt': nrm(ks[0], (BATCH, SEQ, D_MODEL), 1.0),
        'x_sample': nrm(ks[1], (DEC_BATCH, DEC_SEQ, D_MODEL), 1.0),
        'meta_tokens': nrm(ks[2], (N_META, D_MODEL), 1.0),
        'w_in': nrm(ks[3], (DEPTH, D_MODEL, D_IN), D_MODEL ** -0.5),
        'w_out': nrm(ks[4], (DEPTH, D_NA + D_SW, D_MODEL), (D_NA + D_SW) ** -0.5),
        'rpb': nrm(ks[5], (DEPTH, N_NA_HEADS, 2 * NA_MAX_ROWS - 1, 2 * NA_COLS - 1), 0.5),
        'sink': nrm(ks[6], (DEPTH, N_SW_HEADS), 0.5),
        'g_na_out': gain(ks[7], (DEPTH, D_NA)),
        'g_sw_out': gain(ks[8], (DEPTH, D_SW)),
        'g_pre_mix': gain(ks[9], (DEPTH, D_MODEL)),
        'g_post_mix': gain(ks[10], (DEPTH, D_MODEL)),
        'g_pre_ffn': gain(ks[11], (DEPTH, D_MODEL)),
        'g_post_ffn': gain(ks[12], (DEPTH, D_MODEL)),
        'w_gate': nrm(ks[13], (DEPTH, D_MODEL, D_FF), D_MODEL ** -0.5),
        'w_up': nrm(ks[14], (DEPTH, D_MODEL, D_FF), D_MODEL ** -0.5),
        'w_down': nrm(ks[15], (DEPTH, D_FF, D_MODEL), D_FF ** -0.5),
    }


def reference(x_prompt, x_sample, meta_tokens, w_in, w_out, rpb, sink, g_na_out, g_sw_out,
              g_pre_mix, g_post_mix, g_pre_ffn, g_post_ffn, w_gate, w_up, w_down):
    slopes = alibi_slopes(N_SW_HEADS)

    def trunk(x):
        B = x.shape[0]
        meta = jnp.broadcast_to(meta_tokens[None].astype(x.dtype), (B, N_META, D_MODEL))
        h = jnp.concatenate([meta, x], axis=1)
        for l in range(DEPTH):
            h = encoder_layer(h, w_in[l], w_out[l], rpb[l], sink[l], g_na_out[l], g_sw_out[l],
                              g_pre_mix[l], g_post_mix[l], g_pre_ffn[l], g_post_ffn[l],
                              w_gate[l], w_up[l], w_down[l], slopes)
        return h[:, N_META:]

    y_prompt = trunk(x_prompt)
    y_sample = trunk(x_sample)
    return (y_prompt, y_sample)
```

```python
import functools

import numpy as np
import jax
import jax.numpy as jnp
from jax import lax
from jax.experimental import pallas as pl
from jax.experimental.pallas import tpu as pltpu

D_MODEL = 4096
DEPTH = 1
HEAD_DIM = 128
N_HEADS = D_MODEL // HEAD_DIM
N_NA_HEADS = N_HEADS // 2
N_SW_HEADS = N_HEADS - N_NA_HEADS
N_SW_KV_HEADS = max(1, N_SW_HEADS // 4)
SW_GROUP = N_SW_HEADS // N_SW_KV_HEADS
D_NA = N_NA_HEADS * HEAD_DIM
D_SW = N_SW_HEADS * HEAD_DIM
D_SW_KV = N_SW_KV_HEADS * HEAD_DIM
D_IN = 3 * D_NA + D_SW + 2 * D_SW_KV
D_FF = -(-8 * D_MODEL // (3 * 256)) * 256
GRID_W = 64
NA_MAX_ROWS = 8
NA_COLS = 16
NA_QROWS = 2
SW_WINDOW = 128
SW_BLOCK = 128
N_META = 16
EPS = 1e-6
NEG_INF = -1e30

LANES = 128
TOK_BLOCK = NA_QROWS * GRID_W
NA_KEY_ROWS = 10
NA_KEY_BLOCKS = NA_KEY_ROWS * GRID_W // TOK_BLOCK
NA_KEYS = (NA_KEY_BLOCKS + 1) * TOK_BLOCK
SW_KEYS = 4 * SW_BLOCK
D_FF_PAD = 11264
VMEM_LIMIT = 56 * 1024 * 1024

assert SW_BLOCK == TOK_BLOCK and DEPTH == 1


def _cparams(sem):
    return pltpu.CompilerParams(dimension_semantics=sem, vmem_limit_bytes=VMEM_LIMIT)


def _block_tables(seq_blocks):
    kb, pat, prev, nxt, pos, last = [], [], [], [], [], []
    patterns = []
    start_blk = 0
    for nb in seq_blocks:
        rows = nb * NA_QROWS
        assert rows >= NA_KEY_ROWS + 2
        for bl in range(nb):
            r0 = bl * NA_QROWS
            ks = int(np.clip(r0 - NA_MAX_ROWS // 2, 0, rows - NA_KEY_ROWS))
            assert ks % NA_QROWS == 0
            w = tuple(int(np.clip(r0 + a - NA_MAX_ROWS // 2, 0, rows - NA_MAX_ROWS)) - ks for a in range(NA_QROWS))
            assert all(0 <= wa and wa + NA_MAX_ROWS <= NA_KEY_ROWS for wa in w)
            key = (r0 - ks, w)
            if key not in patterns:
                patterns.append(key)
            kb.append(start_blk + ks // NA_QROWS)
            pat.append(patterns.index(key))
            prev.append(start_blk + max(bl - 1, 0))
            nxt.append(start_blk + min(bl + 1, nb - 1))
            pos.append(bl)
            last.append(int(bl == nb - 1))
        start_blk += nb
    i32 = lambda v: jnp.asarray(np.asarray(v, np.int32))
    return dict(kb=i32(kb), pat=i32(pat), prev=i32(prev), nxt=i32(nxt), pos=i32(pos), last=i32(last)), patterns


def _rms(x, g):
    return x * lax.rsqrt(jnp.mean(x * x, axis=-1, keepdims=True) + EPS) * g


def _prenorm2_kernel(xp_ref, xs_ref, g_ref, o_ref, *, n_first):
    i = pl.program_id(0)

    @pl.when(i < n_first)
    def _():
        o_ref[...] = _rms(xp_ref[...], g_ref[...]).astype(o_ref.dtype)

    @pl.when(i >= n_first)
    def _():
        o_ref[...] = _rms(xs_ref[...], g_ref[...]).astype(o_ref.dtype)


def _prenorm2(xp, xs, g, tb=256):
    n_first, n_second = xp.shape[0] // tb, xs.shape[0] // tb
    d = xp.shape[1]
    return pl.pallas_call(
        functools.partial(_prenorm2_kernel, n_first=n_first),
        grid=(n_first + n_second,),
        in_specs=[
            pl.BlockSpec((tb, d), lambda i: (jnp.minimum(i, n_first - 1), 0)),
            pl.BlockSpec((tb, d), lambda i: (jnp.maximum(i - n_first, 0), 0)),
            pl.BlockSpec((1, d), lambda i: (0, 0)),
        ],
        out_specs=pl.BlockSpec((tb, d), lambda i: (i, 0)),
        out_shape=jax.ShapeDtypeStruct((xp.shape[0] + xs.shape[0], d), jnp.bfloat16),
        compiler_params=_cparams(("arbitrary",)),
        name="prenorm",
    )(xp, xs, g)


def _prenorm1_kernel(x_ref, g_ref, o_ref):
    o_ref[...] = _rms(x_ref[...], g_ref[...]).astype(o_ref.dtype)


def _prenorm1(x, g):
    n, d = x.shape
    return pl.pallas_call(
        _prenorm1_kernel,
        grid=(1,),
        in_specs=[pl.BlockSpec((n, d), lambda i: (0, 0)), pl.BlockSpec((1, d), lambda i: (0, 0))],
        out_specs=pl.BlockSpec((n, d), lambda i: (0, 0)),
        out_shape=jax.ShapeDtypeStruct((n, d), jnp.bfloat16),
        compiler_params=_cparams(("arbitrary",)),
        name="prenorm_meta",
    )(x, g)


def _mm_in_kernel(a_ref, b_ref, s_ref, o_ref):
    acc = jnp.dot(a_ref[...], b_ref[...], preferred_element_type=jnp.float32)
    o_ref[...] = (acc * s_ref[...]).astype(o_ref.dtype)


def _mm_in(a, b, colscale, bm, bn=1024, name="mm_in"):
    m, k = a.shape
    n = b.shape[1]
    return pl.pallas_call(
        _mm_in_kernel,
        grid=(m // bm, n // bn),
        in_specs=[
            pl.BlockSpec((bm, k), lambda i, j: (i, 0)),
            pl.BlockSpec((k, bn), lambda i, j: (0, j)),
            pl.BlockSpec((1, bn), lambda i, j: (0, j)),
        ],
        out_specs=pl.BlockSpec((bm, bn), lambda i, j: (i, j)),
        out_shape=jax.ShapeDtypeStruct((m, n), jnp.bfloat16),
        compiler_params=_cparams(("parallel", "parallel")),
        name=name,
    )(a, b, colscale)


def _mm_out_kernel(a1_ref, a2_ref, b1_ref, b2_ref, o_ref):
    acc = jnp.dot(a1_ref[...], b1_ref[...], preferred_element_type=jnp.float32)
    acc = acc + jnp.dot(a2_ref[...], b2_ref[...], preferred_element_type=jnp.float32)
    o_ref[...] = acc


def _mm_out(a1, a2, b, bm=1024, bn=1024):
    m, k1 = a1.shape
    k2 = a2.shape[1]
    n = b.shape[1]
    assert k1 == k2 and b.shape[0] == k1 + k2
    return pl.pallas_call(
        _mm_out_kernel,
        grid=(m // bm, n // bn),
        in_specs=[
            pl.BlockSpec((bm, k1), lambda i, j: (i, 0)),
            pl.BlockSpec((bm, k2), lambda i, j: (i, 0)),
            pl.BlockSpec((k1, bn), lambda i, j: (0, j)),
            pl.BlockSpec((k2, bn), lambda i, j: (1, j)),
        ],
        out_specs=pl.BlockSpec((bm, bn), lambda i, j: (i, j)),
        out_shape=jax.ShapeDtypeStruct((m, n), jnp.float32),
        compiler_params=_cparams(("parallel", "parallel")),
        name="mm_out",
    )(a1, a2, b, b)


def _mm_gateup_kernel(a_ref, bg_ref, bu_ref, o_ref):
    a = a_ref[...]
    g = jnp.dot(a, bg_ref[...], preferred_element_type=jnp.float32)
    u = jnp.dot(a, bu_ref[...], preferred_element_type=jnp.float32)
    o_ref[...] = (g / (1.0 + jnp.exp(-g)) * u).astype(o_ref.dtype)


def _mm_gateup(a, bg, bu, bm=1024, bn=512):
    m, k = a.shape
    n = bg.shape[1]
    return pl.pallas_call(
        _mm_gateup_kernel,
        grid=(m // bm, n // bn),
        in_specs=[
            pl.BlockSpec((bm, k), lambda i, j: (i, 0)),
            pl.BlockSpec((k, bn), lambda i, j: (0, j)),
            pl.BlockSpec((k, bn), lambda i, j: (0, j)),
        ],
        out_specs=pl.BlockSpec((bm, bn), lambda i, j: (i, j)),
        out_shape=jax.ShapeDtypeStruct((m, n), jnp.bfloat16),
        compiler_params=_cparams(("parallel", "parallel")),
        name="mm_gateup",
    )(a, bg, bu)


def _mm_down_kernel(a_ref, b_ref, o_ref, acc_ref):
    k = pl.program_id(2)

    @pl.when(k == 0)
    def _():
        acc_ref[...] = jnp.zeros_like(acc_ref)

    acc_ref[...] += jnp.dot(a_ref[...], b_ref[...], preferred_element_type=jnp.float32)

    @pl.when(k == pl.num_programs(2) - 1)
    def _():
        o_ref[...] = acc_ref[...]


def _mm_down(a, b, bm=1024, bn=1024, bk=2816):
    m, k = a.shape
    n = b.shape[1]
    assert k % bk == 0
    return pl.pallas_call(
        _mm_down_kernel,
        grid=(m // bm, n // bn, k // bk),
        in_specs=[
            pl.BlockSpec((bm, bk), lambda i, j, kk: (i, kk)),
            pl.BlockSpec((bk, bn), lambda i, j, kk: (kk, j)),
        ],
        out_specs=pl.BlockSpec((bm, bn), lambda i, j, kk: (i, j)),
        out_shape=jax.ShapeDtypeStruct((m, n), jnp.float32),
        scratch_shapes=[pltpu.VMEM((bm, bn), jnp.float32)],
        compiler_params=_cparams(("parallel", "parallel", "arbitrary")),
        name="mm_down",
    )(a, b)


def _post1_kernel(xp_ref, xs_ref, mix_ref, gpost_ref, gpre_ref, x1_ref, h_ref, *, n_first):
    i = pl.program_id(0)

    def body(x_ref):
        x1 = x_ref[...] + _rms(mix_ref[...], gpost_ref[...])
        x1_ref[...] = x1
        h_ref[...] = _rms(x1, gpre_ref[...]).astype(h_ref.dtype)

    @pl.when(i < n_first)
    def _():
        body(xp_ref)

    @pl.when(i >= n_first)
    def _():
        body(xs_ref)


def _post1(xp, xs, mix, gpost, gpre, tb=256):
    n_first, n_second = xp.shape[0] // tb, xs.shape[0] // tb
    d = xp.shape[1]
    n = mix.shape[0]
    row = pl.BlockSpec((tb, d), lambda i: (i, 0))
    vec = pl.BlockSpec((1, d), lambda i: (0, 0))
    return pl.pallas_call(
        functools.partial(_post1_kernel, n_first=n_first),
        grid=(n_first + n_second,),
        in_specs=[
            pl.BlockSpec((tb, d), lambda i: (jnp.minimum(i, n_first - 1), 0)),
            pl.BlockSpec((tb, d), lambda i: (jnp.maximum(i - n_first, 0), 0)),
            row, vec, vec,
        ],
        out_specs=[row, row],
        out_shape=[jax.ShapeDtypeStruct((n, d), jnp.float32), jax.ShapeDtypeStruct((n, d), jnp.bfloat16)],
        compiler_params=_cparams(("arbitrary",)),
        name="post_mix",
    )(xp, xs, mix, gpost, gpre)


def _post2_kernel(x1_ref, f_ref, g_ref, o_ref):
    o_ref[...] = x1_ref[...] + _rms(f_ref[...], g_ref[...])


def _post2(x1, f, g, row0, nrows, tb=256):
    d = x1.shape[1]
    b0 = row0 // tb
    src = pl.BlockSpec((tb, d), lambda i: (i + b0, 0))
    return pl.pallas_call(
        _post2_kernel,
        grid=(nrows // tb,),
        in_specs=[src, src, pl.BlockSpec((1, d), lambda i: (0, 0))],
        out_specs=pl.BlockSpec((tb, d), lambda i: (i, 0)),
        out_shape=jax.ShapeDtypeStruct((nrows, d), jnp.float32),
        compiler_params=_cparams(("parallel",)),
        name="post_ffn",
    )(x1, f, g)


def _na_bias_kernel(rpb_ref, o_ref, *, patterns):
    h = pl.program_id(0)
    n_ic = 2 * NA_COLS - 1
    n_ir = 2 * NA_MAX_ROWS - 1
    shape = (GRID_W, LANES)
    qc = lax.broadcasted_iota(jnp.int32, shape, 0)
    lane = lax.broadcasted_iota(jnp.int32, shape, 1)
    kc = lane & (GRID_W - 1)
    delta = kc - qc + (NA_COLS - 1)
    cq = jnp.clip(qc - NA_COLS // 2, 0, GRID_W - NA_COLS)
    col_ok = (kc >= cq) & (kc < cq + NA_COLS)
    left = lane < GRID_W
    neg = jnp.full(shape, NEG_INF, jnp.float32)

    w = []
    for e in range(n_ir):
        acc = jnp.zeros(shape, jnp.float32)
        for d in range(n_ic):
            acc = jnp.where(delta == d, rpb_ref[h * (n_ir * n_ic) + e * n_ic + d], acc)
        w.append(acc)
    zero = jnp.zeros(shape, jnp.float32)

    def tile(dr):
        e = dr + NA_MAX_ROWS - 1
        return w[e] if 0 <= e < n_ir else zero

    for p, (rel, win) in enumerate(patterns):
        for a in range(NA_QROWS):
            for c in range(NA_KEY_BLOCKS):
                b0 = 2 * c
                dr = b0 - a - rel
                ok0 = win[a] <= b0 < win[a] + NA_MAX_ROWS
                ok1 = win[a] <= b0 + 1 < win[a] + NA_MAX_ROWS
                pair = jnp.where(left, tile(dr), tile(dr + 1))
                if ok0 and ok1:
                    ok = col_ok
                elif ok0:
                    ok = col_ok & left
                elif ok1:
                    ok = col_ok & jnp.logical_not(left)
                else:
                    ok = None
                val = neg if ok is None else jnp.where(ok, pair, neg)
                o_ref[p, 0, a * GRID_W:(a + 1) * GRID_W, c * LANES:(c + 1) * LANES] = val
        mshape = (TOK_BLOCK, LANES)
        mlane = lax.broadcasted_iota(jnp.int32, mshape, 1)
        o_ref[p, 0, :, NA_KEY_BLOCKS * LANES:] = jnp.where(mlane < N_META, 0.0, NEG_INF).astype(jnp.float32)


def _na_bias(rpb_flat, patterns):
    npat = len(patterns)
    return pl.pallas_call(
        functools.partial(_na_bias_kernel, patterns=tuple(patterns)),
        grid=(N_NA_HEADS,),
        in_specs=[pl.BlockSpec(memory_space=pltpu.SMEM)],
        out_specs=pl.BlockSpec((npat, 1, TOK_BLOCK, NA_KEYS), lambda h: (0, h, 0, 0)),
        out_shape=jax.ShapeDtypeStruct((npat, N_NA_HEADS, TOK_BLOCK, NA_KEYS), jnp.float32),
        compiler_params=_cparams(("parallel",)),
        name="na_bias",
    )(rpb_flat)


_NT = (((1,), (1,)), ((), ()))


def _group_norm_store(oacc_ref, ss, g_ref, o_ref):
    inv = lax.rsqrt(ss * (1.0 / oacc_ref.shape[1]) + EPS)
    o_ref[...] = (oacc_ref[...] * inv * g_ref[...]).astype(o_ref.dtype)


def _na_kernel(kb_ref, pat_ref, q_ref, *rest):
    k_refs = rest[:NA_KEY_BLOCKS]
    v_refs = rest[NA_KEY_BLOCKS:2 * NA_KEY_BLOCKS]
    km_ref, vm_ref, bias_ref, g_ref, o_ref, oacc_ref = rest[2 * NA_KEY_BLOCKS:]
    ss = jnp.zeros((TOK_BLOCK, 1), jnp.float32)
    for h in range(N_NA_HEADS):
        sl = slice(h * HEAD_DIM, (h + 1) * HEAD_DIM)
        q = q_ref[:, sl]
        k = jnp.concatenate([r[:, sl] for r in k_refs] + [km_ref[:, sl]], axis=0)
        v = jnp.concatenate([r[:, sl] for r in v_refs] + [vm_ref[:, sl]], axis=0)
        s = lax.dot_general(q, k, _NT, preferred_element_type=jnp.float32) + bias_ref[0, h]
        m = jnp.max(s, axis=-1, keepdims=True)
        p = jnp.exp(s - m)
        l = jnp.sum(p, axis=-1, keepdims=True)
        o = jnp.dot(p.astype(v.dtype), v, preferred_element_type=jnp.float32) / l
        ss = ss + jnp.sum(o * o, axis=-1, keepdims=True)
        oacc_ref[:, sl] = o
    _group_norm_store(oacc_ref, ss, g_ref, o_ref)


def _na_attention(proj, projm, bias, g, tables):
    n = proj.shape[0]
    nblk = n // TOK_BLOCK
    blk = (TOK_BLOCK, D_NA)
    in_specs = [pl.BlockSpec(blk, lambda i, kb, pat: (i, 0))]
    in_specs += [pl.BlockSpec(blk, functools.partial(lambda i, kb, pat, j: (kb[i] + j, 1), j=j))
                 for j in range(NA_KEY_BLOCKS)]
    in_specs += [pl.BlockSpec(blk, functools.partial(lambda i, kb, pat, j: (kb[i] + j, 2), j=j))
                 for j in range(NA_KEY_BLOCKS)]
    in_specs += [
        pl.BlockSpec(blk, lambda i, kb, pat: (0, 1)),
        pl.BlockSpec(blk, lambda i, kb, pat: (0, 2)),
        pl.BlockSpec((1, N_NA_HEADS, TOK_BLOCK, NA_KEYS), lambda i, kb, pat: (pat[i], 0, 0, 0)),
        pl.BlockSpec((1, D_NA), lambda i, kb, pat: (0, 0)),
    ]
    return pl.pallas_call(
        _na_kernel,
        grid_spec=pltpu.PrefetchScalarGridSpec(
            num_scalar_prefetch=2,
            grid=(nblk,),
            in_specs=in_specs,
            out_specs=pl.BlockSpec(blk, lambda i, kb, pat: (i, 0)),
            scratch_shapes=[pltpu.VMEM(blk, jnp.float32)],
        ),
        out_shape=jax.ShapeDtypeStruct((n, D_NA), jnp.bfloat16),
        compiler_params=_cparams(("arbitrary",)),
        name="na_attention",
    )(tables["kb"], tables["pat"], proj, *([proj] * (2 * NA_KEY_BLOCKS)), projm, projm, bias, g)


def _swa_kernel(prev_ref, nxt_ref, pos_ref, last_ref, sink_ref, q_ref, kp_ref, kc_ref, kn_ref,
                vp_ref, vc_ref, vn_ref, km_ref, vm_ref, g_ref, o_ref, oacc_ref, *, slopes):
    i = pl.program_id(0)
    pos = pos_ref[i]
    c_lo = jnp.where(pos == 0, SW_BLOCK, 0)
    c_hi = jnp.where(last_ref[i] == 1, 2 * SW_BLOCK, 3 * SW_BLOCK)
    shape = (SW_BLOCK, SW_KEYS)
    r = lax.broadcasted_iota(jnp.int32, shape, 0)
    c = lax.broadcasted_iota(jnp.int32, shape, 1)
    n_real = 3 * SW_BLOCK
    rel = c - r
    valid = (rel >= 0) & (rel <= 2 * SW_WINDOW) & (c >= c_lo) & (c < c_hi)
    is_real = c < n_real
    j = c - n_real
    mdist = jnp.minimum(N_META + pos * SW_BLOCK + r - j, SW_WINDOW)
    dist = jnp.where(is_real, jnp.abs(rel - SW_BLOCK), mdist).astype(jnp.float32)
    valid = (valid & is_real) | ((c >= n_real) & (j < N_META))
    maskadd = jnp.where(valid, 0.0, NEG_INF).astype(jnp.float32)

    ss = jnp.zeros((SW_BLOCK, 1), jnp.float32)
    for kh in range(N_SW_KV_HEADS):
        ksl = slice(kh * HEAD_DIM, (kh + 1) * HEAD_DIM)
        k = jnp.concatenate([kp_ref[:, ksl], kc_ref[:, ksl], kn_ref[:, ksl], km_ref[:, ksl]], axis=0)
        v = jnp.concatenate([vp_ref[:, ksl], vc_ref[:, ksl], vn_ref[:, ksl], vm_ref[:, ksl]], axis=0)
        heads = [kh * SW_GROUP + gi for gi in range(SW_GROUP)]
        q = jnp.concatenate([q_ref[:, h * HEAD_DIM:(h + 1) * HEAD_DIM] for h in heads], axis=0)
        s_all = lax.dot_general(q, k, _NT, preferred_element_type=jnp.float32)
        ps, ls = [], []
        for gi, h in enumerate(heads):
            s = s_all[gi * SW_BLOCK:(gi + 1) * SW_BLOCK] - slopes[h] * dist + maskadd
            sink = sink_ref[h]
            m = jnp.maximum(jnp.max(s, axis=-1, keepdims=True), sink)
            p = jnp.exp(s - m)
            ls.append(jnp.sum(p, axis=-1, keepdims=True) + jnp.exp(sink - m))
            ps.append(p.astype(v.dtype))
        o_all = jnp.dot(jnp.concatenate(ps, axis=0), v, preferred_element_type=jnp.float32)
        for gi, h in enumerate(heads):
            o = o_all[gi * SW_BLOCK:(gi + 1) * SW_BLOCK] / ls[gi]
            ss = ss + jnp.sum(o * o, axis=-1, keepdims=True)
            oacc_ref[:, h * HEAD_DIM:(h + 1) * HEAD_DIM] = o
    _group_norm_store(oacc_ref, ss, g_ref, o_ref)


def _swa_attention(proj, projm, sink, g, tables):
    n = proj.shape[0]
    nblk = n // SW_BLOCK
    qcol = 3 * D_NA // D_SW
    kcol = (3 * D_NA + D_SW) // D_SW_KV
    vcol = kcol + 1
    qblk = (SW_BLOCK, D_SW)
    kvblk = (SW_BLOCK, D_SW_KV)

    def kv(col, which):
        if which == "prev":
            return pl.BlockSpec(kvblk, lambda i, pv, nx, ps, la: (pv[i], col))
        if which == "cur":
            return pl.BlockSpec(kvblk, lambda i, pv, nx, ps, la: (i, col))
        if which == "next":
            return pl.BlockSpec(kvblk, lambda i, pv, nx, ps, la: (nx[i], col))
        return pl.BlockSpec(kvblk, lambda i, pv, nx, ps, la: (0, col))

    slopes = tuple(float(2.0 ** (-8.0 * (h + 1) / N_SW_HEADS)) for h in range(N_SW_HEADS))
    in_specs = [pl.BlockSpec(memory_space=pltpu.SMEM),
                pl.BlockSpec(qblk, lambda i, pv, nx, ps, la: (i, qcol))]
    in_specs += [kv(kcol, w) for w in ("prev", "cur", "next")]
    in_specs += [kv(vcol, w) for w in ("prev", "cur", "next")]
    in_specs += [kv(kcol, "meta"), kv(vcol, "meta"),
                 pl.BlockSpec((1, D_SW), lambda i, pv, nx, ps, la: (0, 0))]
    return pl.pallas_call(
        functools.partial(_swa_kernel, slopes=slopes),
        grid_spec=pltpu.PrefetchScalarGridSpec(
            num_scalar_prefetch=4,
            grid=(nblk,),
            in_specs=in_specs,
            out_specs=pl.BlockSpec(qblk, lambda i, pv, nx, ps, la: (i, 0)),
            scratch_shapes=[pltpu.VMEM(qblk, jnp.float32)],
        ),
        out_shape=jax.ShapeDtypeStruct((n, D_SW), jnp.bfloat16),
        compiler_params=_cparams(("arbitrary",)),
        name="swa_attention",
    )(tables["prev"], tables["nxt"], tables["pos"], tables["last"], sink, proj,
      proj, proj, proj, proj, proj, proj, projm, projm, g)


def kernel(x_prompt, x_sample, meta_tokens, w_in, w_out, rpb, sink, g_na_out, g_sw_out,
           g_pre_mix, g_post_mix, g_pre_ffn, g_post_ffn, w_gate, w_up, w_down):
    bf16 = jnp.bfloat16
    xp = x_prompt.reshape(-1, D_MODEL)
    xs = x_sample.reshape(-1, D_MODEL)
    n_p, n_s = xp.shape[0], xs.shape[0]
    seq_blocks = [x_prompt.shape[1] // TOK_BLOCK] * x_prompt.shape[0]
    seq_blocks += [x_sample.shape[1] // TOK_BLOCK] * x_sample.shape[0]
    tables, patterns = _block_tables(seq_blocks)

    row = lambda v: v.reshape(1, -1)
    w_in_b = w_in[0].astype(bf16)
    w_out_b = w_out[0].astype(bf16)
    ff_pad = D_FF_PAD - D_FF
    w_gate_b = jnp.pad(w_gate[0].astype(bf16), ((0, 0), (0, ff_pad)))
    w_up_b = jnp.pad(w_up[0].astype(bf16), ((0, 0), (0, ff_pad)))
    w_down_b = jnp.pad(w_down[0].astype(bf16), ((0, ff_pad), (0, 0)))

    scale = HEAD_DIM ** -0.5
    colscale = np.ones((1, D_IN), np.float32)
    colscale[:, :D_NA] = scale
    colscale[:, 3 * D_NA:3 * D_NA + D_SW] = scale
    colscale = jnp.asarray(colscale)

    h = _prenorm2(xp, xs, row(g_pre_mix[0]))
    proj = _mm_in(h, w_in_b, colscale, bm=1024)
    meta_pad = jnp.pad(meta_tokens, ((0, TOK_BLOCK - N_META), (0, 0)))
    hm = _prenorm1(meta_pad, row(g_pre_mix[0]))
    projm = _mm_in(hm, w_in_b, colscale, bm=TOK_BLOCK, name="mm_in_meta")

    bias = _na_bias(rpb[0].reshape(-1), patterns)
    o_na = _na_attention(proj, projm, bias, row(g_na_out[0]), tables)
    o_sw = _swa_attention(proj, projm, sink[0], row(g_sw_out[0]), tables)

    mix = _mm_out(o_na, o_sw, w_out_b)
    x1, h2 = _post1(xp, xs, mix, row(g_post_mix[0]), row(g_pre_ffn[0]))

    act = _mm_gateup(h2, w_gate_b, w_up_b)
    f = _mm_down(act, w_down_b)
    y_p = _post2(x1, f, row(g_post_ffn[0]), 0, n_p)
    y_s = _post2(x1, f, row(g_post_ffn[0]), n_p, n_s)
    return (y_p.reshape(x_prompt.shape), y_s.reshape(x_sample.shape))
```

```python
import functools
import math

import numpy as np
import jax
import jax.numpy as jnp
from jax import lax
from jax.experimental import pallas as pl
from jax.experimental.pallas import tpu as pltpu

D_MODEL = 4096
DEPTH = 1
HEAD_DIM = 128
N_HEADS = D_MODEL // HEAD_DIM
N_NA_HEADS = N_HEADS // 2
N_SW_HEADS = N_HEADS - N_NA_HEADS
N_SW_KV_HEADS = max(1, N_SW_HEADS // 4)
SW_GROUP = N_SW_HEADS // N_SW_KV_HEADS
D_NA = N_NA_HEADS * HEAD_DIM
D_SW = N_SW_HEADS * HEAD_DIM
D_SW_KV = N_SW_KV_HEADS * HEAD_DIM
D_IN = 3 * D_NA + D_SW + 2 * D_SW_KV
D_FF = -(-8 * D_MODEL // (3 * 256)) * 256
GRID_W = 64
NA_MAX_ROWS = 8
NA_COLS = 16
NA_QROWS = 2
SW_WINDOW = 128
SW_BLOCK = 128
N_META = 16
EPS = 1e-6
NEG_INF = -1e30

LANES = 128
BF16_SUBLANES = 16
TOK_BLOCK = NA_QROWS * GRID_W
NA_STEP_BLOCKS = 2
NA_STEP_ROWS = NA_STEP_BLOCKS * NA_QROWS
NA_KEY_ROWS = NA_STEP_ROWS + NA_MAX_ROWS
NA_KEY_BLOCKS = NA_KEY_ROWS * GRID_W // TOK_BLOCK
NA_KEYS = NA_KEY_BLOCKS * TOK_BLOCK
SW_KEYS = 4 * SW_BLOCK
FF_MAIN = 10752
FF_TAIL = D_FF - FF_MAIN
VMEM_LIMIT = 56 * 1024 * 1024
LOG2E = math.log2(math.e)

assert SW_BLOCK == TOK_BLOCK and DEPTH == 1 and FF_TAIL == 256


def _cparams(sem, vmem=VMEM_LIMIT):
    return pltpu.CompilerParams(dimension_semantics=sem, vmem_limit_bytes=vmem)


def _block_tables(seq_blocks):
    kb, pat, prev, nxt, pos, last = [], [], [], [], [], []
    patterns = []
    start_blk = 0
    for nb in seq_blocks:
        rows = nb * NA_QROWS
        assert nb % NA_STEP_BLOCKS == 0 and rows >= NA_KEY_ROWS
        for st in range(nb // NA_STEP_BLOCKS):
            r0 = st * NA_STEP_ROWS
            ks = max(r0 - NA_MAX_ROWS // 2, 0)
            assert ks % NA_QROWS == 0
            w = tuple(int(np.clip(r0 + a - NA_MAX_ROWS // 2, 0, rows - NA_MAX_ROWS)) - ks
                      for a in range(NA_STEP_ROWS))
            assert all(0 <= wa and wa + NA_MAX_ROWS <= NA_KEY_ROWS - 1 for wa in w)
            key = (r0 - ks, w)
            if key not in patterns:
                patterns.append(key)
            kb.append(start_blk + ks // NA_QROWS)
            pat.append(patterns.index(key))
        for bl in range(nb):
            prev.append(start_blk + max(bl - 1, 0))
            nxt.append(start_blk + min(bl + 1, nb - 1))
            pos.append(bl)
            last.append(int(bl == nb - 1))
        start_blk += nb
    i32 = lambda v: jnp.asarray(np.asarray(v, np.int32))
    return dict(kb=i32(kb), pat=i32(pat), prev=i32(prev), nxt=i32(nxt), pos=i32(pos), last=i32(last)), patterns


def _rms(x, g):
    return x * lax.rsqrt(jnp.mean(x * x, axis=-1, keepdims=True) + EPS) * g


def _prenorm2_kernel(xp_ref, xs_ref, g_ref, o_ref, *, n_first):
    i = pl.program_id(0)

    @pl.when(i < n_first)
    def _():
        o_ref[...] = _rms(xp_ref[...], g_ref[...]).astype(o_ref.dtype)

    @pl.when(i >= n_first)
    def _():
        o_ref[...] = _rms(xs_ref[...], g_ref[...]).astype(o_ref.dtype)


def _prenorm2(xp, xs, g, tb=256):
    n_first, n_second = xp.shape[0] // tb, xs.shape[0] // tb
    d = xp.shape[1]
    return pl.pallas_call(
        functools.partial(_prenorm2_kernel, n_first=n_first),
        grid=(n_first + n_second,),
        in_specs=[
            pl.BlockSpec((tb, d), lambda i: (jnp.minimum(i, n_first - 1), 0)),
            pl.BlockSpec((tb, d), lambda i: (jnp.maximum(i - n_first, 0), 0)),
            pl.BlockSpec((1, d), lambda i: (0, 0)),
        ],
        out_specs=pl.BlockSpec((tb, d), lambda i: (i, 0)),
        out_shape=jax.ShapeDtypeStruct((xp.shape[0] + xs.shape[0], d), jnp.bfloat16),
        compiler_params=_cparams(("arbitrary",)),
        name="prenorm",
    )(xp, xs, g)


def _prenorm1_kernel(x_ref, g_ref, o_ref):
    o_ref[...] = _rms(x_ref[...], g_ref[...]).astype(o_ref.dtype)


def _prenorm1(x, g):
    n, d = x.shape
    return pl.pallas_call(
        _prenorm1_kernel,
        grid=(1,),
        in_specs=[pl.BlockSpec((n, d), lambda i: (0, 0)), pl.BlockSpec((1, d), lambda i: (0, 0))],
        out_specs=pl.BlockSpec((n, d), lambda i: (0, 0)),
        out_shape=jax.ShapeDtypeStruct((n, d), jnp.bfloat16),
        compiler_params=_cparams(("arbitrary",)),
        name="prenorm_meta",
    )(x, g)


def _mm_in_kernel(a_ref, b_ref, s_ref, o_ref):
    acc = jnp.dot(a_ref[...], b_ref[...].astype(jnp.bfloat16), preferred_element_type=jnp.float32)
    o_ref[...] = (acc * s_ref[...]).astype(o_ref.dtype)


def _mm_in(a, b, colscale, bm, bn=1024, name="mm_in"):
    m, k = a.shape
    n = b.shape[1]
    return pl.pallas_call(
        _mm_in_kernel,
        grid=(m // bm, n // bn),
        in_specs=[
            pl.BlockSpec((bm, k), lambda i, j: (i, 0)),
            pl.BlockSpec((k, bn), lambda i, j: (0, j)),
            pl.BlockSpec((1, bn), lambda i, j: (0, j)),
        ],
        out_specs=pl.BlockSpec((bm, bn), lambda i, j: (i, j)),
        out_shape=jax.ShapeDtypeStruct((m, n), jnp.bfloat16),
        compiler_params=_cparams(("parallel", "parallel"),
                                 2 * bm * k * 2 + 2 * k * bn * 4 + k * bn * 2 + 2 * bm * bn * 2 + (2 << 20)),
        name=name,
    )(a, b, colscale)


def _mm_out_kernel(a1_ref, a2_ref, b1_ref, b2_ref, o_ref):
    acc = jnp.dot(a1_ref[...], b1_ref[...].astype(jnp.bfloat16), preferred_element_type=jnp.float32)
    acc = acc + jnp.dot(a2_ref[...], b2_ref[...].astype(jnp.bfloat16), preferred_element_type=jnp.float32)
    o_ref[...] = acc.astype(o_ref.dtype)


def _mm_out(a1, a2, b, bm=1024, bn=512):
    m, k1 = a1.shape
    k2 = a2.shape[1]
    n = b.shape[1]
    assert k1 == k2 and b.shape[0] == k1 + k2
    return pl.pallas_call(
        _mm_out_kernel,
        grid=(m // bm, n // bn),
        in_specs=[
            pl.BlockSpec((bm, k1), lambda i, j: (i, 0)),
            pl.BlockSpec((bm, k2), lambda i, j: (i, 0)),
            pl.BlockSpec((k1, bn), lambda i, j: (0, j)),
            pl.BlockSpec((k2, bn), lambda i, j: (1, j)),
        ],
        out_specs=pl.BlockSpec((bm, bn), lambda i, j: (i, j)),
        out_shape=jax.ShapeDtypeStruct((m, n), jnp.bfloat16),
        compiler_params=_cparams(("parallel", "parallel")),
        name="mm_out",
    )(a1, a2, b, b)


def _mm_gateup_kernel(a_ref, bg_ref, bu_ref, o_ref):
    a = a_ref[...]
    g = jnp.dot(a, bg_ref[...].astype(jnp.bfloat16), preferred_element_type=jnp.float32)
    u = jnp.dot(a, bu_ref[...].astype(jnp.bfloat16), preferred_element_type=jnp.float32)
    o_ref[...] = (g / (1.0 + jnp.exp(-g)) * u).astype(o_ref.dtype)


def _mm_gateup(a, bg, bu, col0, ncols, bn, name, bm=1024):
    m, k = a.shape
    assert col0 % bn == 0 and ncols % bn == 0
    j0 = col0 // bn
    wspec = pl.BlockSpec((k, bn), lambda i, j: (0, j + j0))
    return pl.pallas_call(
        _mm_gateup_kernel,
        grid=(m // bm, ncols // bn),
        in_specs=[pl.BlockSpec((bm, k), lambda i, j: (i, 0)), wspec, wspec],
        out_specs=pl.BlockSpec((bm, bn), lambda i, j: (i, j)),
        out_shape=jax.ShapeDtypeStruct((m, ncols), jnp.bfloat16),
        compiler_params=_cparams(("parallel", "parallel"),
                                 2 * bm * k * 2 + 4 * k * bn * 4 + 2 * k * bn * 2 + 2 * bm * bn * 2 + (2 << 20)),
        name=name,
    )(a, bg, bu)


def _mm_down_kernel(a1_ref, a2_ref, b1_ref, b2_ref, o_ref):
    acc = jnp.dot(a1_ref[...], b1_ref[...], preferred_element_type=jnp.float32)
    acc = acc + jnp.dot(a2_ref[...], b2_ref[...], preferred_element_type=jnp.float32)
    o_ref[...] = acc.astype(o_ref.dtype)


def _mm_down(a1, a2, b, bm=512, bn=512):
    m, k1 = a1.shape
    k2 = a2.shape[1]
    n = b.shape[1]
    assert b.shape[0] == k1 + k2 and k1 % k2 == 0
    return pl.pallas_call(
        _mm_down_kernel,
        grid=(m // bm, n // bn),
        in_specs=[
            pl.BlockSpec((bm, k1), lambda i, j: (i, 0)),
            pl.BlockSpec((bm, k2), lambda i, j: (i, 0)),
            pl.BlockSpec((k1, bn), lambda i, j: (0, j)),
            pl.BlockSpec((k2, bn), lambda i, j: (k1 // k2, j)),
        ],
        out_specs=pl.BlockSpec((bm, bn), lambda i, j: (i, j)),
        out_shape=jax.ShapeDtypeStruct((m, n), jnp.bfloat16),
        compiler_params=_cparams(("parallel", "parallel")),
        name="mm_down",
    )(a1, a2, b, b)


def _post1_kernel(xp_ref, xs_ref, mix_ref, gpost_ref, gpre_ref, x1_ref, h_ref, *, n_first):
    i = pl.program_id(0)

    def body(x_ref):
        x1 = x_ref[...] + _rms(mix_ref[...].astype(jnp.float32), gpost_ref[...])
        x1_ref[...] = x1
        h_ref[...] = _rms(x1, gpre_ref[...]).astype(h_ref.dtype)

    @pl.when(i < n_first)
    def _():
        body(xp_ref)

    @pl.when(i >= n_first)
    def _():
        body(xs_ref)


def _post1(xp, xs, mix, gpost, gpre, tb=256):
    n_first, n_second = xp.shape[0] // tb, xs.shape[0] // tb
    d = xp.shape[1]
    n = mix.shape[0]
    row = pl.BlockSpec((tb, d), lambda i: (i, 0))
    vec = pl.BlockSpec((1, d), lambda i: (0, 0))
    return pl.pallas_call(
        functools.partial(_post1_kernel, n_first=n_first),
        grid=(n_first + n_second,),
        in_specs=[
            pl.BlockSpec((tb, d), lambda i: (jnp.minimum(i, n_first - 1), 0)),
            pl.BlockSpec((tb, d), lambda i: (jnp.maximum(i - n_first, 0), 0)),
            row, vec, vec,
        ],
        out_specs=[row, row],
        out_shape=[jax.ShapeDtypeStruct((n, d), jnp.float32), jax.ShapeDtypeStruct((n, d), jnp.bfloat16)],
        compiler_params=_cparams(("arbitrary",)),
        name="post_mix",
    )(xp, xs, mix, gpost, gpre)


def _post2_kernel(x1_ref, f_ref, g_ref, o_ref):
    o_ref[...] = x1_ref[...] + _rms(f_ref[...].astype(jnp.float32), g_ref[...])


def _post2(x1, f, g, row0, nrows, tb=256):
    d = x1.shape[1]
    b0 = row0 // tb
    src = pl.BlockSpec((tb, d), lambda i: (i + b0, 0))
    return pl.pallas_call(
        _post2_kernel,
        grid=(nrows // tb,),
        in_specs=[src, src, pl.BlockSpec((1, d), lambda i: (0, 0))],
        out_specs=pl.BlockSpec((tb, d), lambda i: (i, 0)),
        out_shape=jax.ShapeDtypeStruct((nrows, d), jnp.float32),
        compiler_params=_cparams(("parallel",)),
        name="post_ffn",
    )(x1, f, g)


def _na_bias_kernel(rpb_ref, o_ref, *, patterns):
    h = pl.program_id(0)
    n_ic = 2 * NA_COLS - 1
    n_ir = 2 * NA_MAX_ROWS - 1
    shape = (GRID_W, LANES)
    qc = lax.broadcasted_iota(jnp.int32, shape, 0)
    lane = lax.broadcasted_iota(jnp.int32, shape, 1)
    kc = lane & (GRID_W - 1)
    delta = kc - qc + (NA_COLS - 1)
    cq = jnp.clip(qc - NA_COLS // 2, 0, GRID_W - NA_COLS)
    col_ok = (kc >= cq) & (kc < cq + NA_COLS)
    left = lane < GRID_W
    neg = jnp.full(shape, NEG_INF, jnp.float32)

    w = []
    for e in range(n_ir):
        acc = jnp.zeros(shape, jnp.float32)
        for d in range(n_ic):
            acc = jnp.where(delta == d, rpb_ref[h * (n_ir * n_ic) + e * n_ic + d], acc)
        w.append(acc * LOG2E)
    zero = jnp.zeros(shape, jnp.float32)

    def tile(dr):
        e = dr + NA_MAX_ROWS - 1
        return w[e] if 0 <= e < n_ir else zero

    for p, (rel, win) in enumerate(patterns):
        for a in range(NA_STEP_ROWS):
            for c in range(NA_KEY_BLOCKS):
                b0 = 2 * c
                dr = b0 - a - rel
                ok0 = win[a] <= b0 < win[a] + NA_MAX_ROWS
                ok1 = win[a] <= b0 + 1 < win[a] + NA_MAX_ROWS
                pair = jnp.where(left, tile(dr), tile(dr + 1))
                if ok0 and ok1:
                    ok = col_ok
                elif ok0:
                    ok = col_ok & left
                elif ok1:
                    ok = col_ok & jnp.logical_not(left)
                else:
                    ok = None
                val = neg if ok is None else jnp.where(ok, pair, neg)
                if c == NA_KEY_BLOCKS - 1:
                    assert not ok1
                    val = jnp.where(lane >= LANES - N_META, 0.0, val)
                o_ref[p, 0, a * GRID_W:(a + 1) * GRID_W, c * LANES:(c + 1) * LANES] = val


def _na_bias(rpb_flat, patterns):
    npat = len(patterns)
    q = NA_STEP_BLOCKS * TOK_BLOCK
    return pl.pallas_call(
        functools.partial(_na_bias_kernel, patterns=tuple(patterns)),
        grid=(N_NA_HEADS,),
        in_specs=[pl.BlockSpec(memory_space=pltpu.SMEM)],
        out_specs=pl.BlockSpec((npat, 1, q, NA_KEYS), lambda h: (0, h, 0, 0)),
        out_shape=jax.ShapeDtypeStruct((npat, N_NA_HEADS, q, NA_KEYS), jnp.float32),
        compiler_params=_cparams(("parallel",)),
        name="na_bias",
    )(rpb_flat)


_NT = (((1,), (1,)), ((), ()))


def _group_norm_store(oacc_ref, ss, g_ref, o_ref):
    inv = lax.rsqrt(ss * (1.0 / oacc_ref.shape[1]) + EPS)
    o_ref[...] = (oacc_ref[...] * inv * g_ref[...]).astype(o_ref.dtype)


def _na_kernel(kb_ref, pat_ref, q_ref, *rest):
    k_refs = rest[:NA_KEY_BLOCKS]
    v_refs = rest[NA_KEY_BLOCKS:2 * NA_KEY_BLOCKS]
    km_ref, vm_ref, bias_ref, g_ref, o_ref, oacc_ref = rest[2 * NA_KEY_BLOCKS:]
    nq = q_ref.shape[0]
    real = TOK_BLOCK - BF16_SUBLANES
    ss = jnp.zeros((nq, 1), jnp.float32)
    for h in range(N_NA_HEADS):
        sl = slice(h * HEAD_DIM, (h + 1) * HEAD_DIM)
        q = q_ref[:, sl]
        k = jnp.concatenate([r[:, sl] for r in k_refs[:-1]]
                            + [k_refs[-1][:real, sl], km_ref[real:, sl]], axis=0)
        v = jnp.concatenate([r[:, sl] for r in v_refs[:-1]]
                            + [v_refs[-1][:real, sl], vm_ref[real:, sl]], axis=0)
        s = lax.dot_general(q, k, _NT, preferred_element_type=jnp.float32) + bias_ref[0, h]
        m = jnp.max(s, axis=-1, keepdims=True)
        p = jnp.exp2(s - m)
        l = jnp.sum(p, axis=-1, keepdims=True)
        o = jnp.dot(p.astype(v.dtype), v, preferred_element_type=jnp.float32) / l
        ss = ss + jnp.sum(o * o, axis=-1, keepdims=True)
        oacc_ref[:, sl] = o
    _group_norm_store(oacc_ref, ss, g_ref, o_ref)


def _na_attention(proj, projm, bias, g, tables):
    n = proj.shape[0]
    nblk = n // TOK_BLOCK
    nq = NA_STEP_BLOCKS * TOK_BLOCK
    qblk = (nq, D_NA)
    kblk = (TOK_BLOCK, D_NA)

    def kv_spec(col, j):
        return pl.BlockSpec(kblk, lambda i, kb, pat: (jnp.minimum(kb[i] + j, nblk - 1), col))

    in_specs = [pl.BlockSpec(qblk, lambda i, kb, pat: (i, 0))]
    in_specs += [kv_spec(1, j) for j in range(NA_KEY_BLOCKS)]
    in_specs += [kv_spec(2, j) for j in range(NA_KEY_BLOCKS)]
    in_specs += [
        pl.BlockSpec(kblk, lambda i, kb, pat: (0, 1)),
        pl.BlockSpec(kblk, lambda i, kb, pat: (0, 2)),
        pl.BlockSpec((1, N_NA_HEADS, nq, NA_KEYS), lambda i, kb, pat: (pat[i], 0, 0, 0)),
        pl.BlockSpec((1, D_NA), lambda i, kb, pat: (0, 0)),
    ]
    return pl.pallas_call(
        _na_kernel,
        grid_spec=pltpu.PrefetchScalarGridSpec(
            num_scalar_prefetch=2,
            grid=(n // nq,),
            in_specs=in_specs,
            out_specs=pl.BlockSpec(qblk, lambda i, kb, pat: (i, 0)),
            scratch_shapes=[pltpu.VMEM(qblk, jnp.float32)],
        ),
        out_shape=jax.ShapeDtypeStruct((n, D_NA), jnp.bfloat16),
        compiler_params=_cparams(("arbitrary",)),
        name="na_attention",
    )(tables["kb"], tables["pat"], proj, *([proj] * (2 * NA_KEY_BLOCKS)), projm, projm, bias, g)


def _swa_kernel(prev_ref, nxt_ref, pos_ref, last_ref, sink_ref, q_ref, kp_ref, kc_ref, kn_ref,
                vp_ref, vc_ref, vn_ref, km_ref, vm_ref, g_ref, o_ref, oacc_ref, *, slopes):
    i = pl.program_id(0)
    pos = pos_ref[i]
    c_lo = jnp.where(pos == 0, SW_BLOCK, 0)
    c_hi = jnp.where(last_ref[i] == 1, 2 * SW_BLOCK, 3 * SW_BLOCK)
    shape = (SW_BLOCK, SW_KEYS)
    r = lax.broadcasted_iota(jnp.int32, shape, 0)
    c = lax.broadcasted_iota(jnp.int32, shape, 1)
    n_real = 3 * SW_BLOCK
    rel = c - r
    valid = (rel >= 0) & (rel <= 2 * SW_WINDOW) & (c >= c_lo) & (c < c_hi)
    is_real = c < n_real
    j = c - n_real
    mdist = jnp.minimum(N_META + pos * SW_BLOCK + r - j, SW_WINDOW)
    dist = jnp.where(is_real, jnp.abs(rel - SW_BLOCK), mdist).astype(jnp.float32)
    valid = (valid & is_real) | ((c >= n_real) & (j < N_META))
    maskadd = jnp.where(valid, 0.0, NEG_INF).astype(jnp.float32)

    ss = jnp.zeros((SW_BLOCK, 1), jnp.float32)
    for kh in range(N_SW_KV_HEADS):
        ksl = slice(kh * HEAD_DIM, (kh + 1) * HEAD_DIM)
        k = jnp.concatenate([kp_ref[:, ksl], kc_ref[:, ksl], kn_ref[:, ksl], km_ref[:, ksl]], axis=0)
        v = jnp.concatenate([vp_ref[:, ksl], vc_ref[:, ksl], vn_ref[:, ksl], vm_ref[:, ksl]], axis=0)
        heads = [kh * SW_GROUP + gi for gi in range(SW_GROUP)]
        q = jnp.concatenate([q_ref[:, h * HEAD_DIM:(h + 1) * HEAD_DIM] for h in heads], axis=0)
        s_all = lax.dot_general(q, k, _NT, preferred_element_type=jnp.float32)
        ps, ls = [], []
        for gi, h in enumerate(heads):
            s = s_all[gi * SW_BLOCK:(gi + 1) * SW_BLOCK] - (slopes[h] * LOG2E) * dist + maskadd
            sink = sink_ref[h] * LOG2E
            m = jnp.maximum(jnp.max(s, axis=-1, keepdims=True), sink)
            p = jnp.exp2(s - m)
            ls.append(jnp.sum(p, axis=-1, keepdims=True) + jnp.exp2(sink - m))
            ps.append(p.astype(v.dtype))
        o_all = jnp.dot(jnp.concatenate(ps, axis=0), v, preferred_element_type=jnp.float32)
        for gi, h in enumerate(heads):
            o = o_all[gi * SW_BLOCK:(gi + 1) * SW_BLOCK] / ls[gi]
            ss = ss + jnp.sum(o * o, axis=-1, keepdims=True)
            oacc_ref[:, h * HEAD_DIM:(h + 1) * HEAD_DIM] = o
    _group_norm_store(oacc_ref, ss, g_ref, o_ref)


def _swa_attention(proj, projm, sink, g, tables):
    n = proj.shape[0]
    nblk = n // SW_BLOCK
    qcol = 3 * D_NA // D_SW
    kcol = (3 * D_NA + D_SW) // D_SW_KV
    vcol = kcol + 1
    qblk = (SW_BLOCK, D_SW)
    kvblk = (SW_BLOCK, D_SW_KV)

    def kv(col, which):
        if which == "prev":
            return pl.BlockSpec(kvblk, lambda i, pv, nx, ps, la: (pv[i], col))
        if which == "cur":
            return pl.BlockSpec(kvblk, lambda i, pv, nx, ps, la: (i, col))
        if which == "next":
            return pl.BlockSpec(kvblk, lambda i, pv, nx, ps, la: (nx[i], col))
        return pl.BlockSpec(kvblk, lambda i, pv, nx, ps, la: (0, col))

    slopes = tuple(float(2.0 ** (-8.0 * (h + 1) / N_SW_HEADS)) for h in range(N_SW_HEADS))
    in_specs = [pl.BlockSpec(memory_space=pltpu.SMEM),
                pl.BlockSpec(qblk, lambda i, pv, nx, ps, la: (i, qcol))]
    in_specs += [kv(kcol, w) for w in ("prev", "cur", "next")]
    in_specs += [kv(vcol, w) for w in ("prev", "cur", "next")]
    in_specs += [kv(kcol, "meta"), kv(vcol, "meta"),
                 pl.BlockSpec((1, D_SW), lambda i, pv, nx, ps, la: (0, 0))]
    return pl.pallas_call(
        functools.partial(_swa_kernel, slopes=slopes),
        grid_spec=pltpu.PrefetchScalarGridSpec(
            num_scalar_prefetch=4,
            grid=(nblk,),
            in_specs=in_specs,
            out_specs=pl.BlockSpec(qblk, lambda i, pv, nx, ps, la: (i, 0)),
            scratch_shapes=[pltpu.VMEM(qblk, jnp.float32)],
        ),
        out_shape=jax.ShapeDtypeStruct((n, D_SW), jnp.bfloat16),
        compiler_params=_cparams(("arbitrary",)),
        name="swa_attention",
    )(tables["prev"], tables["nxt"], tables["pos"], tables["last"], sink, proj,
      proj, proj, proj, proj, proj, proj, projm, projm, g)


def kernel(x_prompt, x_sample, meta_tokens, w_in, w_out, rpb, sink, g_na_out, g_sw_out,
           g_pre_mix, g_post_mix, g_pre_ffn, g_post_ffn, w_gate, w_up, w_down):
    xp = x_prompt.reshape(-1, D_MODEL)
    xs = x_sample.reshape(-1, D_MODEL)
    n_p, n_s = xp.shape[0], xs.shape[0]
    seq_blocks = [x_prompt.shape[1] // TOK_BLOCK] * x_prompt.shape[0]
    seq_blocks += [x_sample.shape[1] // TOK_BLOCK] * x_sample.shape[0]
    tables, patterns = _block_tables(seq_blocks)
    row = lambda v: v.reshape(1, -1)

    scale = HEAD_DIM ** -0.5 * LOG2E
    colscale = np.ones((1, D_IN), np.float32)
    colscale[:, :D_NA] = scale
    colscale[:, 3 * D_NA:3 * D_NA + D_SW] = scale
    colscale = jnp.asarray(colscale)

    h = _prenorm2(xp, xs, row(g_pre_mix[0]))
    proj = _mm_in(h, w_in[0], colscale, bm=1024)
    gap = jnp.zeros((TOK_BLOCK - 2 * N_META, D_MODEL), meta_tokens.dtype)
    meta_blk = jnp.concatenate([meta_tokens, gap, meta_tokens], axis=0)
    hm = _prenorm1(meta_blk, row(g_pre_mix[0]))
    projm = _mm_in(hm, w_in[0], colscale, bm=TOK_BLOCK, name="mm_in_meta")

    bias = _na_bias(rpb[0].reshape(-1), patterns)
    o_na = _na_attention(proj, projm, bias, row(g_na_out[0]), tables)
    o_sw = _swa_attention(proj, projm, sink[0], row(g_sw_out[0]), tables)

    mix = _mm_out(o_na, o_sw, w_out[0])
    x1, h2 = _post1(xp, xs, mix, row(g_post_mix[0]), row(g_pre_ffn[0]))

    act_main = _mm_gateup(h2, w_gate[0], w_up[0], 0, FF_MAIN, 512, "mm_gateup")
    act_tail = _mm_gateup(h2, w_gate[0], w_up[0], FF_MAIN, FF_TAIL, FF_TAIL, "mm_gateup_tail")
    f = _mm_down(act_main, act_tail, w_down[0].astype(jnp.bfloat16))
    y_p = _post2(x1, f, row(g_post_ffn[0]), 0, n_p)
    y_s = _post2(x1, f, row(g_post_ffn[0]), n_p, n_s)
    return (y_p.reshape(x_prompt.shape), y_s.reshape(x_sample.shape))
```

```python
import functools
import math

import numpy as np
import jax
import jax.numpy as jnp
from jax import lax
from jax.experimental import pallas as pl
from jax.experimental.pallas import tpu as pltpu

D_MODEL = 4096
DEPTH = 1
HEAD_DIM = 128
N_HEADS = D_MODEL // HEAD_DIM
N_NA_HEADS = N_HEADS // 2
N_SW_HEADS = N_HEADS - N_NA_HEADS
N_SW_KV_HEADS = max(1, N_SW_HEADS // 4)
SW_GROUP = N_SW_HEADS // N_SW_KV_HEADS
D_NA = N_NA_HEADS * HEAD_DIM
D_SW = N_SW_HEADS * HEAD_DIM
D_SW_KV = N_SW_KV_HEADS * HEAD_DIM
D_IN = 3 * D_NA + D_SW + 2 * D_SW_KV
D_FF = -(-8 * D_MODEL // (3 * 256)) * 256
GRID_W = 64
NA_MAX_ROWS = 8
NA_COLS = 16
NA_QROWS = 2
SW_WINDOW = 128
SW_BLOCK = 128
N_META = 16
EPS = 1e-6
NEG_INF = -1e30

LANES = 128
BF16_SUBLANES = 16
TOK_BLOCK = NA_QROWS * GRID_W
NA_STEP_BLOCKS = 2
NA_STEP_ROWS = NA_STEP_BLOCKS * NA_QROWS
NA_KEY_ROWS = NA_STEP_ROWS + NA_MAX_ROWS
NA_KEY_BLOCKS = NA_KEY_ROWS * GRID_W // TOK_BLOCK
NA_KEYS = NA_KEY_BLOCKS * TOK_BLOCK
SW_KEYS = 4 * SW_BLOCK
FF_MAIN = 10752
FF_TAIL = D_FF - FF_MAIN
VMEM_LIMIT = 56 * 1024 * 1024
LOG2E = math.log2(math.e)

assert SW_BLOCK == TOK_BLOCK and DEPTH == 1 and FF_TAIL == 256


def _cparams(sem, vmem=VMEM_LIMIT):
    return pltpu.CompilerParams(dimension_semantics=sem, vmem_limit_bytes=vmem)


def _block_tables(seq_blocks):
    kb, pat, prev, nxt, swpat = [], [], [], [], []
    patterns = []
    start_blk = 0
    for nb in seq_blocks:
        rows = nb * NA_QROWS
        assert nb % NA_STEP_BLOCKS == 0 and rows >= NA_KEY_ROWS
        for st in range(nb // NA_STEP_BLOCKS):
            r0 = st * NA_STEP_ROWS
            ks = max(r0 - NA_MAX_ROWS // 2, 0)
            assert ks % NA_QROWS == 0
            w = tuple(int(np.clip(r0 + a - NA_MAX_ROWS // 2, 0, rows - NA_MAX_ROWS)) - ks
                      for a in range(NA_STEP_ROWS))
            assert all(0 <= wa and wa + NA_MAX_ROWS <= NA_KEY_ROWS - 1 for wa in w)
            key = (r0 - ks, w)
            if key not in patterns:
                patterns.append(key)
            kb.append(start_blk + ks // NA_QROWS)
            pat.append(patterns.index(key))
        for bl in range(nb):
            prev.append(start_blk + max(bl - 1, 0))
            nxt.append(start_blk + min(bl + 1, nb - 1))
            swpat.append(int(bl == 0) + 2 * int(bl == nb - 1))
        start_blk += nb
    i32 = lambda v: jnp.asarray(np.asarray(v, np.int32))
    return dict(kb=i32(kb), pat=i32(pat), prev=i32(prev), nxt=i32(nxt), swpat=i32(swpat)), patterns


def _rms(x, g):
    return x * lax.rsqrt(jnp.mean(x * x, axis=-1, keepdims=True) + EPS) * g


def _prenorm2_kernel(xp_ref, xs_ref, g_ref, o_ref, *, n_first):
    i = pl.program_id(0)

    @pl.when(i < n_first)
    def _():
        o_ref[...] = _rms(xp_ref[...], g_ref[...]).astype(o_ref.dtype)

    @pl.when(i >= n_first)
    def _():
        o_ref[...] = _rms(xs_ref[...], g_ref[...]).astype(o_ref.dtype)


def _prenorm2(xp, xs, g, tb=256):
    n_first, n_second = xp.shape[0] // tb, xs.shape[0] // tb
    d = xp.shape[1]
    return pl.pallas_call(
        functools.partial(_prenorm2_kernel, n_first=n_first),
        grid=(n_first + n_second,),
        in_specs=[
            pl.BlockSpec((tb, d), lambda i: (jnp.minimum(i, n_first - 1), 0)),
            pl.BlockSpec((tb, d), lambda i: (jnp.maximum(i - n_first, 0), 0)),
            pl.BlockSpec((1, d), lambda i: (0, 0)),
        ],
        out_specs=pl.BlockSpec((tb, d), lambda i: (i, 0)),
        out_shape=jax.ShapeDtypeStruct((xp.shape[0] + xs.shape[0], d), jnp.bfloat16),
        compiler_params=_cparams(("arbitrary",)),
        name="prenorm",
    )(xp, xs, g)


def _prenorm1_kernel(x_ref, g_ref, o_ref):
    o_ref[...] = _rms(x_ref[...], g_ref[...]).astype(o_ref.dtype)


def _prenorm1(x, g):
    n, d = x.shape
    return pl.pallas_call(
        _prenorm1_kernel,
        grid=(1,),
        in_specs=[pl.BlockSpec((n, d), lambda i: (0, 0)), pl.BlockSpec((1, d), lambda i: (0, 0))],
        out_specs=pl.BlockSpec((n, d), lambda i: (0, 0)),
        out_shape=jax.ShapeDtypeStruct((n, d), jnp.bfloat16),
        compiler_params=_cparams(("arbitrary",)),
        name="prenorm_meta",
    )(x, g)


def _mm_in_kernel(a_ref, b_ref, s_ref, o_ref):
    acc = jnp.dot(a_ref[...], b_ref[...].astype(jnp.bfloat16), preferred_element_type=jnp.float32)
    o_ref[...] = (acc * s_ref[...]).astype(o_ref.dtype)


def _mm_in(a, b, colscale, bm, bn=1024, name="mm_in", col_steps=None, col_of=lambda j: j):
    m, k = a.shape
    n = b.shape[1]
    return pl.pallas_call(
        _mm_in_kernel,
        grid=(m // bm, n // bn if col_steps is None else col_steps),
        in_specs=[
            pl.BlockSpec((bm, k), lambda i, j: (i, 0)),
            pl.BlockSpec((k, bn), lambda i, j: (0, col_of(j))),
            pl.BlockSpec((1, bn), lambda i, j: (0, col_of(j))),
        ],
        out_specs=pl.BlockSpec((bm, bn), lambda i, j: (i, col_of(j))),
        out_shape=jax.ShapeDtypeStruct((m, n), jnp.bfloat16),
        compiler_params=_cparams(("parallel", "parallel"),
                                 2 * bm * k * 2 + 2 * k * bn * 4 + k * bn * 2 + 2 * bm * bn * 2 + (2 << 20)),
        name=name,
    )(a, b, colscale)


def _mm_out_kernel(a1_ref, a2_ref, b1_ref, b2_ref, o_ref):
    acc = jnp.dot(a1_ref[...], b1_ref[...].astype(jnp.bfloat16), preferred_element_type=jnp.float32)
    acc = acc + jnp.dot(a2_ref[...], b2_ref[...].astype(jnp.bfloat16), preferred_element_type=jnp.float32)
    o_ref[...] = acc.astype(o_ref.dtype)


def _mm_out(a1, a2, b, bm=1024, bn=1024):
    m, k1 = a1.shape
    k2 = a2.shape[1]
    n = b.shape[1]
    assert k1 == k2 and b.shape[0] == k1 + k2
    vmem = 4 * bm * k1 * 2 + 4 * k1 * bn * 4 + 2 * k1 * bn * 2 + 2 * bm * bn * 2 + (2 << 20)
    return pl.pallas_call(
        _mm_out_kernel,
        grid=(m // bm, n // bn),
        in_specs=[
            pl.BlockSpec((bm, k1), lambda i, j: (i, 0)),
            pl.BlockSpec((bm, k2), lambda i, j: (i, 0)),
            pl.BlockSpec((k1, bn), lambda i, j: (0, j)),
            pl.BlockSpec((k2, bn), lambda i, j: (1, j)),
        ],
        out_specs=pl.BlockSpec((bm, bn), lambda i, j: (i, j)),
        out_shape=jax.ShapeDtypeStruct((m, n), jnp.bfloat16),
        compiler_params=_cparams(("parallel", "parallel"), vmem),
        name="mm_out",
    )(a1, a2, b, b)


def _mm_gateup_kernel(a_ref, bg_ref, bu_ref, o_ref):
    a = a_ref[...]
    g = jnp.dot(a, bg_ref[...].astype(jnp.bfloat16), preferred_element_type=jnp.float32)
    u = jnp.dot(a, bu_ref[...].astype(jnp.bfloat16), preferred_element_type=jnp.float32)
    o_ref[...] = (g / (1.0 + jnp.exp(-g)) * u).astype(o_ref.dtype)


def _mm_gateup(a, bg, bu, col0, ncols, bn, name, bm=1024):
    m, k = a.shape
    assert col0 % bn == 0 and ncols % bn == 0
    j0 = col0 // bn
    wspec = pl.BlockSpec((k, bn), lambda i, j: (0, j + j0))
    return pl.pallas_call(
        _mm_gateup_kernel,
        grid=(m // bm, ncols // bn),
        in_specs=[pl.BlockSpec((bm, k), lambda i, j: (i, 0)), wspec, wspec],
        out_specs=pl.BlockSpec((bm, bn), lambda i, j: (i, j)),
        out_shape=jax.ShapeDtypeStruct((m, ncols), jnp.bfloat16),
        compiler_params=_cparams(("parallel", "parallel"),
                                 2 * bm * k * 2 + 4 * k * bn * 4 + 2 * k * bn * 2 + 2 * bm * bn * 2 + (2 << 20)),
        name=name,
    )(a, bg, bu)


def _mm_down_kernel(a1_ref, a2_ref, b1_ref, b2_ref, o_ref, acc_ref):
    kk = pl.program_id(2)

    @pl.when(kk == 0)
    def _():
        acc_ref[...] = jnp.dot(a2_ref[...], b2_ref[...], preferred_element_type=jnp.float32)

    acc_ref[...] += jnp.dot(a1_ref[...], b1_ref[...], preferred_element_type=jnp.float32)

    @pl.when(kk == pl.num_programs(2) - 1)
    def _():
        o_ref[...] = acc_ref[...].astype(o_ref.dtype)


def _mm_down(a1, a2, b, bm=1024, bn=1024, ksteps=3):
    m, k1 = a1.shape
    k2 = a2.shape[1]
    n = b.shape[1]
    bk = k1 // ksteps
    assert b.shape[0] == k1 + k2 and k1 % k2 == 0 and k1 % ksteps == 0 and bk % LANES == 0
    return pl.pallas_call(
        _mm_down_kernel,
        grid=(m // bm, n // bn, ksteps),
        in_specs=[
            pl.BlockSpec((bm, bk), lambda i, j, kk: (i, kk)),
            pl.BlockSpec((bm, k2), lambda i, j, kk: (i, 0)),
            pl.BlockSpec((bk, bn), lambda i, j, kk: (kk, j)),
            pl.BlockSpec((k2, bn), lambda i, j, kk: (k1 // k2, j)),
        ],
        out_specs=pl.BlockSpec((bm, bn), lambda i, j, kk: (i, j)),
        out_shape=jax.ShapeDtypeStruct((m, n), jnp.bfloat16),
        scratch_shapes=[pltpu.VMEM((bm, bn), jnp.float32)],
        compiler_params=_cparams(("parallel", "parallel", "arbitrary")),
        name="mm_down",
    )(a1, a2, b, b)


def _post1_kernel(xp_ref, xs_ref, mix_ref, gpost_ref, gpre_ref, x1_ref, h_ref, *, n_first):
    i = pl.program_id(0)

    def body(x_ref):
        x1 = x_ref[...] + _rms(mix_ref[...].astype(jnp.float32), gpost_ref[...])
        x1_ref[...] = x1
        h_ref[...] = _rms(x1, gpre_ref[...]).astype(h_ref.dtype)

    @pl.when(i < n_first)
    def _():
        body(xp_ref)

    @pl.when(i >= n_first)
    def _():
        body(xs_ref)


def _post1(xp, xs, mix, gpost, gpre, tb=256):
    n_first, n_second = xp.shape[0] // tb, xs.shape[0] // tb
    d = xp.shape[1]
    n = mix.shape[0]
    row = pl.BlockSpec((tb, d), lambda i: (i, 0))
    vec = pl.BlockSpec((1, d), lambda i: (0, 0))
    return pl.pallas_call(
        functools.partial(_post1_kernel, n_first=n_first),
        grid=(n_first + n_second,),
        in_specs=[
            pl.BlockSpec((tb, d), lambda i: (jnp.minimum(i, n_first - 1), 0)),
            pl.BlockSpec((tb, d), lambda i: (jnp.maximum(i - n_first, 0), 0)),
            row, vec, vec,
        ],
        out_specs=[row, row],
        out_shape=[jax.ShapeDtypeStruct((n, d), jnp.float32), jax.ShapeDtypeStruct((n, d), jnp.bfloat16)],
        compiler_params=_cparams(("arbitrary",)),
        name="post_mix",
    )(xp, xs, mix, gpost, gpre)


def _post2_kernel(x1_ref, f_ref, g_ref, o_ref):
    o_ref[...] = x1_ref[...] + _rms(f_ref[...].astype(jnp.float32), g_ref[...])


def _post2(x1, f, g, row0, nrows, tb=256):
    d = x1.shape[1]
    b0 = row0 // tb
    src = pl.BlockSpec((tb, d), lambda i: (i + b0, 0))
    return pl.pallas_call(
        _post2_kernel,
        grid=(nrows // tb,),
        in_specs=[src, src, pl.BlockSpec((1, d), lambda i: (0, 0))],
        out_specs=pl.BlockSpec((tb, d), lambda i: (i, 0)),
        out_shape=jax.ShapeDtypeStruct((nrows, d), jnp.float32),
        compiler_params=_cparams(("parallel",)),
        name="post_ffn",
    )(x1, f, g)


def _na_bias_kernel(rpb_ref, o_ref, *, patterns):
    h = pl.program_id(0)
    n_ic = 2 * NA_COLS - 1
    n_ir = 2 * NA_MAX_ROWS - 1
    shape = (GRID_W, LANES)
    qc = lax.broadcasted_iota(jnp.int32, shape, 0)
    lane = lax.broadcasted_iota(jnp.int32, shape, 1)
    kc = lane & (GRID_W - 1)
    delta = kc - qc + (NA_COLS - 1)
    cq = jnp.clip(qc - NA_COLS // 2, 0, GRID_W - NA_COLS)
    col_ok = (kc >= cq) & (kc < cq + NA_COLS)
    left = lane < GRID_W
    neg = jnp.full(shape, NEG_INF, jnp.float32)

    w = []
    for e in range(n_ir):
        acc = jnp.zeros(shape, jnp.float32)
        for d in range(n_ic):
            acc = jnp.where(delta == d, rpb_ref[h * (n_ir * n_ic) + e * n_ic + d], acc)
        w.append(acc * LOG2E)
    zero = jnp.zeros(shape, jnp.float32)

    def tile(dr):
        e = dr + NA_MAX_ROWS - 1
        return w[e] if 0 <= e < n_ir else zero

    for p, (rel, win) in enumerate(patterns):
        for a in range(NA_STEP_ROWS):
            for c in range(NA_KEY_BLOCKS):
                b0 = 2 * c
                dr = b0 - a - rel
                ok0 = win[a] <= b0 < win[a] + NA_MAX_ROWS
                ok1 = win[a] <= b0 + 1 < win[a] + NA_MAX_ROWS
                pair = jnp.where(left, tile(dr), tile(dr + 1))
                if ok0 and ok1:
                    ok = col_ok
                elif ok0:
                    ok = col_ok & left
                elif ok1:
                    ok = col_ok & jnp.logical_not(left)
                else:
                    ok = None
                val = neg if ok is None else jnp.where(ok, pair, neg)
                if c == NA_KEY_BLOCKS - 1:
                    assert not ok1
                    val = jnp.where(lane >= LANES - N_META, 0.0, val)
                o_ref[p, 0, a * GRID_W:(a + 1) * GRID_W, c * LANES:(c + 1) * LANES] = val


def _na_bias(rpb_flat, patterns):
    npat = len(patterns)
    q = NA_STEP_BLOCKS * TOK_BLOCK
    return pl.pallas_call(
        functools.partial(_na_bias_kernel, patterns=tuple(patterns)),
        grid=(N_NA_HEADS,),
        in_specs=[pl.BlockSpec(memory_space=pltpu.SMEM)],
        out_specs=pl.BlockSpec((npat, 1, q, NA_KEYS), lambda h: (0, h, 0, 0)),
        out_shape=jax.ShapeDtypeStruct((npat, N_NA_HEADS, q, NA_KEYS), jnp.float32),
        compiler_params=_cparams(("parallel",)),
        name="na_bias",
    )(rpb_flat)


_NT = (((1,), (1,)), ((), ()))


def _group_norm_store(oacc_ref, ss, g_ref, o_ref):
    inv = lax.rsqrt(ss * (1.0 / oacc_ref.shape[1]) + EPS)
    o_ref[...] = (oacc_ref[...] * inv * g_ref[...]).astype(o_ref.dtype)


def _na_kernel(kb_ref, pat_ref, q_ref, *rest):
    k_refs = rest[:NA_KEY_BLOCKS]
    v_refs = rest[NA_KEY_BLOCKS:2 * NA_KEY_BLOCKS]
    km_ref, vm_ref, bias_ref, g_ref, o_ref, oacc_ref, sa_ref, sb_ref = rest[2 * NA_KEY_BLOCKS:]
    nq = q_ref.shape[0]
    real = TOK_BLOCK - BF16_SUBLANES

    def with_meta(refs, m_ref, sl):
        return jnp.concatenate([r[:, sl] for r in refs[:-1]] + [refs[-1][:real, sl], m_ref[real:, sl]], axis=0)

    def logits(h, s_ref):
        sl = slice(h * HEAD_DIM, (h + 1) * HEAD_DIM)
        k = with_meta(k_refs, km_ref, sl)
        s_ref[...] = lax.dot_general(q_ref[:, sl], k, _NT, preferred_element_type=jnp.float32) + bias_ref[0, h]

    def softmax_pv(h, s_ref, ss):
        sl = slice(h * HEAD_DIM, (h + 1) * HEAD_DIM)
        s = s_ref[...]
        m = jnp.max(s, axis=-1, keepdims=True)
        p = jnp.exp2(s - m)
        l = jnp.sum(p, axis=-1, keepdims=True)
        v = with_meta(v_refs, vm_ref, sl)
        o = jnp.dot(p.astype(v.dtype), v, preferred_element_type=jnp.float32) / l
        oacc_ref[:, sl] = o
        return ss + jnp.sum(o * o, axis=-1, keepdims=True)

    bufs = (sa_ref, sb_ref)
    ss = jnp.zeros((nq, 1), jnp.float32)
    logits(0, bufs[0])
    for h in range(N_NA_HEADS):
        if h + 1 < N_NA_HEADS:
            logits(h + 1, bufs[(h + 1) % 2])
        ss = softmax_pv(h, bufs[h % 2], ss)
    _group_norm_store(oacc_ref, ss, g_ref, o_ref)


def _na_attention(proj, projm, bias, g, tables):
    n = proj.shape[0]
    nblk = n // TOK_BLOCK
    nq = NA_STEP_BLOCKS * TOK_BLOCK
    qblk = (nq, D_NA)
    kblk = (TOK_BLOCK, D_NA)

    def kv_spec(col, j):
        return pl.BlockSpec(kblk, lambda i, kb, pat: (jnp.minimum(kb[i] + j, nblk - 1), col))

    in_specs = [pl.BlockSpec(qblk, lambda i, kb, pat: (i, 0))]
    in_specs += [kv_spec(1, j) for j in range(NA_KEY_BLOCKS)]
    in_specs += [kv_spec(2, j) for j in range(NA_KEY_BLOCKS)]
    in_specs += [
        pl.BlockSpec(kblk, lambda i, kb, pat: (0, 1)),
        pl.BlockSpec(kblk, lambda i, kb, pat: (0, 2)),
        pl.BlockSpec((1, N_NA_HEADS, nq, NA_KEYS), lambda i, kb, pat: (pat[i], 0, 0, 0)),
        pl.BlockSpec((1, D_NA), lambda i, kb, pat: (0, 0)),
    ]
    return pl.pallas_call(
        _na_kernel,
        grid_spec=pltpu.PrefetchScalarGridSpec(
            num_scalar_prefetch=2,
            grid=(n // nq,),
            in_specs=in_specs,
            out_specs=pl.BlockSpec(qblk, lambda i, kb, pat: (i, 0)),
            scratch_shapes=[pltpu.VMEM(qblk, jnp.float32),
                            pltpu.VMEM((nq, NA_KEYS), jnp.float32),
                            pltpu.VMEM((nq, NA_KEYS), jnp.float32)],
        ),
        out_shape=jax.ShapeDtypeStruct((n, D_NA), jnp.bfloat16),
        compiler_params=_cparams(("arbitrary",)),
        name="na_attention",
    )(tables["kb"], tables["pat"], proj, *([proj] * (2 * NA_KEY_BLOCKS)), projm, projm, bias, g)


def _swa_bias_kernel(slope_ref, o_ref):
    slope = slope_ref[pl.program_id(0)]
    shape = (SW_BLOCK, SW_KEYS)
    r = lax.broadcasted_iota(jnp.int32, shape, 0)
    c = lax.broadcasted_iota(jnp.int32, shape, 1)
    n_real = 3 * SW_BLOCK
    rel = c - r
    band = (rel >= 0) & (rel <= 2 * SW_WINDOW)
    is_real = c < n_real
    j = c - n_real
    is_meta = (c >= n_real) & (j < N_META)
    for p in range(4):
        first, last = p & 1, p >> 1
        c_lo = SW_BLOCK if first else 0
        c_hi = 2 * SW_BLOCK if last else 3 * SW_BLOCK
        valid = (band & (c >= c_lo) & (c < c_hi)) | is_meta
        mdist = jnp.minimum(N_META + r - j, SW_WINDOW) if first else jnp.full(shape, SW_WINDOW, jnp.int32)
        dist = jnp.where(is_real, jnp.abs(rel - SW_BLOCK), mdist).astype(jnp.float32)
        o_ref[p, 0] = jnp.where(valid, -slope * dist, NEG_INF)


def _swa_bias(slopes):
    return pl.pallas_call(
        _swa_bias_kernel,
        grid=(N_SW_HEADS,),
        in_specs=[pl.BlockSpec(memory_space=pltpu.SMEM)],
        out_specs=pl.BlockSpec((4, 1, SW_BLOCK, SW_KEYS), lambda h: (0, h, 0, 0)),
        out_shape=jax.ShapeDtypeStruct((4, N_SW_HEADS, SW_BLOCK, SW_KEYS), jnp.float32),
        compiler_params=_cparams(("parallel",)),
        name="swa_bias",
    )(slopes)


def _swa_kernel(prev_ref, nxt_ref, pat_ref, sink_ref, q_ref, kp_ref, kc_ref, kn_ref,
                vp_ref, vc_ref, vn_ref, km_ref, vm_ref, bias_ref, g_ref, o_ref, oacc_ref, sa_ref, sb_ref):
    def group_heads(kh):
        return [kh * SW_GROUP + gi for gi in range(SW_GROUP)]

    def logits(kh, s_ref):
        ksl = slice(kh * HEAD_DIM, (kh + 1) * HEAD_DIM)
        k = jnp.concatenate([kp_ref[:, ksl], kc_ref[:, ksl], kn_ref[:, ksl], km_ref[:, ksl]], axis=0)
        q = jnp.concatenate([q_ref[:, h * HEAD_DIM:(h + 1) * HEAD_DIM] for h in group_heads(kh)], axis=0)
        s_all = lax.dot_general(q, k, _NT, preferred_element_type=jnp.float32)
        for gi, h in enumerate(group_heads(kh)):
            rows = slice(gi * SW_BLOCK, (gi + 1) * SW_BLOCK)
            s_ref[rows] = s_all[rows] + bias_ref[0, h]

    def softmax_pv(kh, s_ref, ss):
        ksl = slice(kh * HEAD_DIM, (kh + 1) * HEAD_DIM)
        v = jnp.concatenate([vp_ref[:, ksl], vc_ref[:, ksl], vn_ref[:, ksl], vm_ref[:, ksl]], axis=0)
        ps, ls = [], []
        for gi, h in enumerate(group_heads(kh)):
            s = s_ref[gi * SW_BLOCK:(gi + 1) * SW_BLOCK]
            sink = sink_ref[h] * LOG2E
            m = jnp.maximum(jnp.max(s, axis=-1, keepdims=True), sink)
            p = jnp.exp2(s - m)
            ls.append(jnp.sum(p, axis=-1, keepdims=True) + jnp.exp2(sink - m))
            ps.append(p.astype(v.dtype))
        o_all = jnp.dot(jnp.concatenate(ps, axis=0), v, preferred_element_type=jnp.float32)
        for gi, h in enumerate(group_heads(kh)):
            o = o_all[gi * SW_BLOCK:(gi + 1) * SW_BLOCK] / ls[gi]
            ss = ss + jnp.sum(o * o, axis=-1, keepdims=True)
            oacc_ref[:, h * HEAD_DIM:(h + 1) * HEAD_DIM] = o
        return ss

    bufs = (sa_ref, sb_ref)
    ss = jnp.zeros((SW_BLOCK, 1), jnp.float32)
    logits(0, bufs[0])
    for kh in range(N_SW_KV_HEADS):
        if kh + 1 < N_SW_KV_HEADS:
            logits(kh + 1, bufs[(kh + 1) % 2])
        ss = softmax_pv(kh, bufs[kh % 2], ss)
    _group_norm_store(oacc_ref, ss, g_ref, o_ref)


def _swa_attention(proj, projm, sink, bias, g, tables):
    n = proj.shape[0]
    nblk = n // SW_BLOCK
    qcol = 3 * D_NA // D_SW
    kcol = (3 * D_NA + D_SW) // D_SW_KV
    vcol = kcol + 1
    qblk = (SW_BLOCK, D_SW)
    kvblk = (SW_BLOCK, D_SW_KV)

    def kv(col, which):
        if which == "prev":
            return pl.BlockSpec(kvblk, lambda i, pv, nx, pt: (pv[i], col))
        if which == "cur":
            return pl.BlockSpec(kvblk, lambda i, pv, nx, pt: (i, col))
        if which == "next":
            return pl.BlockSpec(kvblk, lambda i, pv, nx, pt: (nx[i], col))
        return pl.BlockSpec(kvblk, lambda i, pv, nx, pt: (0, col))

    in_specs = [pl.BlockSpec(memory_space=pltpu.SMEM),
                pl.BlockSpec(qblk, lambda i, pv, nx, pt: (i, qcol))]
    in_specs += [kv(kcol, w) for w in ("prev", "cur", "next")]
    in_specs += [kv(vcol, w) for w in ("prev", "cur", "next")]
    in_specs += [kv(kcol, "meta"), kv(vcol, "meta"),
                 pl.BlockSpec((1, N_SW_HEADS, SW_BLOCK, SW_KEYS), lambda i, pv, nx, pt: (pt[i], 0, 0, 0)),
                 pl.BlockSpec((1, D_SW), lambda i, pv, nx, pt: (0, 0))]
    return pl.pallas_call(
        _swa_kernel,
        grid_spec=pltpu.PrefetchScalarGridSpec(
            num_scalar_prefetch=3,
            grid=(nblk,),
            in_specs=in_specs,
            out_specs=pl.BlockSpec(qblk, lambda i, pv, nx, pt: (i, 0)),
            scratch_shapes=[pltpu.VMEM(qblk, jnp.float32),
                            pltpu.VMEM((SW_GROUP * SW_BLOCK, SW_KEYS), jnp.float32),
                            pltpu.VMEM((SW_GROUP * SW_BLOCK, SW_KEYS), jnp.float32)],
        ),
        out_shape=jax.ShapeDtypeStruct((n, D_SW), jnp.bfloat16),
        compiler_params=_cparams(("arbitrary",)),
        name="swa_attention",
    )(tables["prev"], tables["nxt"], tables["swpat"], sink, proj,
      proj, proj, proj, proj, proj, proj, projm, projm, bias, g)


def kernel(x_prompt, x_sample, meta_tokens, w_in, w_out, rpb, sink, g_na_out, g_sw_out,
           g_pre_mix, g_post_mix, g_pre_ffn, g_post_ffn, w_gate, w_up, w_down):
    xp = x_prompt.reshape(-1, D_MODEL)
    xs = x_sample.reshape(-1, D_MODEL)
    n_p, n_s = xp.shape[0], xs.shape[0]
    seq_blocks = [x_prompt.shape[1] // TOK_BLOCK] * x_prompt.shape[0]
    seq_blocks += [x_sample.shape[1] // TOK_BLOCK] * x_sample.shape[0]
    tables, patterns = _block_tables(seq_blocks)
    row = lambda v: v.reshape(1, -1)

    scale = HEAD_DIM ** -0.5 * LOG2E
    colscale = np.ones((1, D_IN), np.float32)
    colscale[:, :D_NA] = scale
    colscale[:, 3 * D_NA:3 * D_NA + D_SW] = scale
    colscale = jnp.asarray(colscale)

    h = _prenorm2(xp, xs, row(g_pre_mix[0]))
    proj = _mm_in(h, w_in[0], colscale, bm=1024)
    gap = jnp.zeros((TOK_BLOCK - 2 * N_META, D_MODEL), meta_tokens.dtype)
    meta_blk = jnp.concatenate([meta_tokens, gap, meta_tokens], axis=0)
    hm = _prenorm1(meta_blk, row(g_pre_mix[0]))
    bn = 2 * D_SW_KV
    n_kv, kv0, sw_kv = 2 * D_NA // bn, D_NA // bn, (3 * D_NA + D_SW) // bn
    projm = _mm_in(hm, w_in[0], colscale, bm=TOK_BLOCK, bn=bn, name="mm_in_meta", col_steps=n_kv + 1,
                   col_of=lambda j: jnp.where(j < n_kv, j + kv0, sw_kv))

    bias = _na_bias(rpb[0].reshape(-1), patterns)
    o_na = _na_attention(proj, projm, bias, row(g_na_out[0]), tables)
    slopes = np.asarray([2.0 ** (-8.0 * (i + 1) / N_SW_HEADS) * LOG2E for i in range(N_SW_HEADS)], np.float32)
    o_sw = _swa_attention(proj, projm, sink[0], _swa_bias(jnp.asarray(slopes)), row(g_sw_out[0]), tables)

    mix = _mm_out(o_na, o_sw, w_out[0])
    x1, h2 = _post1(xp, xs, mix, row(g_post_mix[0]), row(g_pre_ffn[0]))

    act_main = _mm_gateup(h2, w_gate[0], w_up[0], 0, FF_MAIN, 512, "mm_gateup")
    act_tail = _mm_gateup(h2, w_gate[0], w_up[0], FF_MAIN, FF_TAIL, FF_TAIL, "mm_gateup_tail")
    f = _mm_down(act_main, act_tail, w_down[0].astype(jnp.bfloat16))
    y_p = _post2(x1, f, row(g_post_ffn[0]), 0, n_p)
    y_s = _post2(x1, f, row(g_post_ffn[0]), n_p, n_s)
    return (y_p.reshape(x_prompt.shape), y_s.reshape(x_sample.shape))
```

```python
import functools
import math

import numpy as np
import jax
import jax.numpy as jnp
from jax import lax
from jax.experimental import pallas as pl
from jax.experimental.pallas import tpu as pltpu

D_MODEL = 4096
DEPTH = 1
HEAD_DIM = 128
N_HEADS = D_MODEL // HEAD_DIM
N_NA_HEADS = N_HEADS // 2
N_SW_HEADS = N_HEADS - N_NA_HEADS
N_SW_KV_HEADS = max(1, N_SW_HEADS // 4)
SW_GROUP = N_SW_HEADS // N_SW_KV_HEADS
D_NA = N_NA_HEADS * HEAD_DIM
D_SW = N_SW_HEADS * HEAD_DIM
D_SW_KV = N_SW_KV_HEADS * HEAD_DIM
D_IN = 3 * D_NA + D_SW + 2 * D_SW_KV
D_FF = -(-8 * D_MODEL // (3 * 256)) * 256
GRID_W = 64
NA_MAX_ROWS = 8
NA_COLS = 16
NA_QROWS = 2
SW_WINDOW = 128
SW_BLOCK = 128
N_META = 16
EPS = 1e-6
NEG_INF = -1e30

LANES = 128
BF16_SUBLANES = 16
TOK_BLOCK = NA_QROWS * GRID_W
NA_STEP_BLOCKS = 2
NA_STEP_ROWS = NA_STEP_BLOCKS * NA_QROWS
NA_KEY_ROWS = NA_STEP_ROWS + NA_MAX_ROWS
NA_KEY_BLOCKS = NA_KEY_ROWS * GRID_W // TOK_BLOCK
NA_KEYS = NA_KEY_BLOCKS * TOK_BLOCK
SW_KEYS = 4 * SW_BLOCK
FF_MAIN = 10752
FF_TAIL = D_FF - FF_MAIN
VMEM_LIMIT = 56 * 1024 * 1024
LOG2E = math.log2(math.e)

assert SW_BLOCK == TOK_BLOCK and DEPTH == 1 and FF_TAIL == 256


def _cparams(sem, vmem=VMEM_LIMIT):
    return pltpu.CompilerParams(dimension_semantics=sem, vmem_limit_bytes=vmem)


def _block_tables(seq_blocks):
    kb, pat, prev, nxt, swpat = [], [], [], [], []
    patterns = []
    start_blk = 0
    for nb in seq_blocks:
        rows = nb * NA_QROWS
        assert nb % NA_STEP_BLOCKS == 0 and rows >= NA_KEY_ROWS
        for st in range(nb // NA_STEP_BLOCKS):
            r0 = st * NA_STEP_ROWS
            ks = max(r0 - NA_MAX_ROWS // 2, 0)
            assert ks % NA_QROWS == 0
            w = tuple(int(np.clip(r0 + a - NA_MAX_ROWS // 2, 0, rows - NA_MAX_ROWS)) - ks
                      for a in range(NA_STEP_ROWS))
            assert all(0 <= wa and wa + NA_MAX_ROWS <= NA_KEY_ROWS - 1 for wa in w)
            key = (r0 - ks, w)
            if key not in patterns:
                patterns.append(key)
            kb.append(start_blk + ks // NA_QROWS)
            pat.append(patterns.index(key))
        for bl in range(nb):
            prev.append(start_blk + max(bl - 1, 0))
            nxt.append(start_blk + min(bl + 1, nb - 1))
            swpat.append(int(bl == 0) + 2 * int(bl == nb - 1))
        start_blk += nb
    i32 = lambda v: jnp.asarray(np.asarray(v, np.int32))
    return dict(kb=i32(kb), pat=i32(pat), prev=i32(prev), nxt=i32(nxt), swpat=i32(swpat)), patterns


def _rms(x, g):
    return x * lax.rsqrt(jnp.mean(x * x, axis=-1, keepdims=True) + EPS) * g


def _prenorm2_kernel(xp_ref, xs_ref, g_ref, o_ref, *, n_first):
    i = pl.program_id(0)

    @pl.when(i < n_first)
    def _():
        o_ref[...] = _rms(xp_ref[...], g_ref[...]).astype(o_ref.dtype)

    @pl.when(i >= n_first)
    def _():
        o_ref[...] = _rms(xs_ref[...], g_ref[...]).astype(o_ref.dtype)


def _prenorm2(xp, xs, g, tb=256):
    n_first, n_second = xp.shape[0] // tb, xs.shape[0] // tb
    d = xp.shape[1]
    return pl.pallas_call(
        functools.partial(_prenorm2_kernel, n_first=n_first),
        grid=(n_first + n_second,),
        in_specs=[
            pl.BlockSpec((tb, d), lambda i: (jnp.minimum(i, n_first - 1), 0)),
            pl.BlockSpec((tb, d), lambda i: (jnp.maximum(i - n_first, 0), 0)),
            pl.BlockSpec((1, d), lambda i: (0, 0)),
        ],
        out_specs=pl.BlockSpec((tb, d), lambda i: (i, 0)),
        out_shape=jax.ShapeDtypeStruct((xp.shape[0] + xs.shape[0], d), jnp.bfloat16),
        compiler_params=_cparams(("arbitrary",)),
        name="prenorm",
    )(xp, xs, g)


def _prenorm1_kernel(x_ref, g_ref, o_ref):
    o_ref[...] = _rms(x_ref[...], g_ref[...]).astype(o_ref.dtype)


def _prenorm1(x, g):
    n, d = x.shape
    return pl.pallas_call(
        _prenorm1_kernel,
        grid=(1,),
        in_specs=[pl.BlockSpec((n, d), lambda i: (0, 0)), pl.BlockSpec((1, d), lambda i: (0, 0))],
        out_specs=pl.BlockSpec((n, d), lambda i: (0, 0)),
        out_shape=jax.ShapeDtypeStruct((n, d), jnp.bfloat16),
        compiler_params=_cparams(("arbitrary",)),
        name="prenorm_meta",
    )(x, g)


def _mm_in_kernel(a_ref, b_ref, s_ref, o_ref):
    acc = jnp.dot(a_ref[...], b_ref[...].astype(jnp.bfloat16), preferred_element_type=jnp.float32)
    o_ref[...] = (acc * s_ref[...]).astype(o_ref.dtype)


def _mm_in(a, b, colscale, bm, bn=1024, name="mm_in", col_steps=None, col_of=lambda j: j):
    m, k = a.shape
    col_steps = b.shape[1] // bn if col_steps is None else col_steps
    return pl.pallas_call(
        _mm_in_kernel,
        grid=(m // bm, col_steps),
        in_specs=[
            pl.BlockSpec((bm, k), lambda i, j: (i, 0)),
            pl.BlockSpec((k, bn), lambda i, j: (0, col_of(j))),
            pl.BlockSpec((1, bn), lambda i, j: (0, col_of(j))),
        ],
        out_specs=pl.BlockSpec((bm, bn), lambda i, j: (i, j)),
        out_shape=jax.ShapeDtypeStruct((m, col_steps * bn), jnp.bfloat16),
        compiler_params=_cparams(("parallel", "parallel"),
                                 2 * bm * k * 2 + 2 * k * bn * 4 + k * bn * 2 + 2 * bm * bn * 2 + (2 << 20)),
        name=name,
    )(a, b, colscale)


def _mm_out_kernel(a1_ref, a2_ref, b1_ref, b2_ref, o_ref):
    acc = jnp.dot(a1_ref[...], b1_ref[...].astype(jnp.bfloat16), preferred_element_type=jnp.float32)
    acc = acc + jnp.dot(a2_ref[...], b2_ref[...].astype(jnp.bfloat16), preferred_element_type=jnp.float32)
    o_ref[...] = acc.astype(o_ref.dtype)


def _mm_out(a1, a2, b, bm=1024, bn=1024):
    m, k1 = a1.shape
    k2 = a2.shape[1]
    n = b.shape[1]
    assert k1 == k2 and b.shape[0] == k1 + k2
    vmem = 4 * bm * k1 * 2 + 4 * k1 * bn * 4 + 2 * k1 * bn * 2 + 2 * bm * bn * 2 + (2 << 20)
    return pl.pallas_call(
        _mm_out_kernel,
        grid=(m // bm, n // bn),
        in_specs=[
            pl.BlockSpec((bm, k1), lambda i, j: (i, 0)),
            pl.BlockSpec((bm, k2), lambda i, j: (i, 0)),
            pl.BlockSpec((k1, bn), lambda i, j: (0, j)),
            pl.BlockSpec((k2, bn), lambda i, j: (1, j)),
        ],
        out_specs=pl.BlockSpec((bm, bn), lambda i, j: (i, j)),
        out_shape=jax.ShapeDtypeStruct((m, n), jnp.bfloat16),
        compiler_params=_cparams(("parallel", "parallel"), vmem),
        name="mm_out",
    )(a1, a2, b, b)


def _mm_gateup_kernel(a_ref, bg_ref, bu_ref, o_ref):
    a = a_ref[...]
    g = jnp.dot(a, bg_ref[...].astype(jnp.bfloat16), preferred_element_type=jnp.float32)
    u = jnp.dot(a, bu_ref[...].astype(jnp.bfloat16), preferred_element_type=jnp.float32)
    o_ref[...] = (g / (1.0 + jnp.exp(-g)) * u).astype(o_ref.dtype)


def _mm_gateup(a, bg, bu, col0, ncols, bn, name, bm=1024):
    m, k = a.shape
    assert col0 % bn == 0 and ncols % bn == 0
    j0 = col0 // bn
    wspec = pl.BlockSpec((k, bn), lambda i, j: (0, j + j0))
    return pl.pallas_call(
        _mm_gateup_kernel,
        grid=(m // bm, ncols // bn),
        in_specs=[pl.BlockSpec((bm, k), lambda i, j: (i, 0)), wspec, wspec],
        out_specs=pl.BlockSpec((bm, bn), lambda i, j: (i, j)),
        out_shape=jax.ShapeDtypeStruct((m, ncols), jnp.bfloat16),
        compiler_params=_cparams(("parallel", "parallel"),
                                 2 * bm * k * 2 + 4 * k * bn * 4 + 2 * k * bn * 2 + 2 * bm * bn * 2 + (2 << 20)),
        name=name,
    )(a, bg, bu)


def _mm_down_kernel(a1_ref, a2_ref, b1_ref, b2_ref, o_ref, acc_ref):
    kk = pl.program_id(2)

    @pl.when(kk == 0)
    def _():
        acc_ref[...] = jnp.dot(a2_ref[...], b2_ref[...].astype(jnp.bfloat16), preferred_element_type=jnp.float32)

    acc_ref[...] += jnp.dot(a1_ref[...], b1_ref[...].astype(jnp.bfloat16), preferred_element_type=jnp.float32)

    @pl.when(kk == pl.num_programs(2) - 1)
    def _():
        o_ref[...] = acc_ref[...].astype(o_ref.dtype)


def _mm_down(a1, a2, b, bm=1024, bn=1024, ksteps=3):
    m, k1 = a1.shape
    k2 = a2.shape[1]
    n = b.shape[1]
    bk = k1 // ksteps
    assert b.shape[0] == k1 + k2 and k1 % k2 == 0 and k1 % ksteps == 0 and bk % LANES == 0
    vmem = (2 * bm * (bk + k2) * 2 + 2 * (bk + k2) * bn * 4 + (bk + k2) * bn * 2 + 2 * bm * bn * 2
            + bm * bn * 4 + (2 << 20))
    return pl.pallas_call(
        _mm_down_kernel,
        grid=(m // bm, n // bn, ksteps),
        in_specs=[
            pl.BlockSpec((bm, bk), lambda i, j, kk: (i, kk)),
            pl.BlockSpec((bm, k2), lambda i, j, kk: (i, 0)),
            pl.BlockSpec((bk, bn), lambda i, j, kk: (kk, j)),
            pl.BlockSpec((k2, bn), lambda i, j, kk: (k1 // k2, j)),
        ],
        out_specs=pl.BlockSpec((bm, bn), lambda i, j, kk: (i, j)),
        out_shape=jax.ShapeDtypeStruct((m, n), jnp.bfloat16),
        scratch_shapes=[pltpu.VMEM((bm, bn), jnp.float32)],
        compiler_params=_cparams(("parallel", "parallel", "arbitrary"), vmem),
        name="mm_down",
    )(a1, a2, b, b)


def _post1_kernel(xp_ref, xs_ref, mix_ref, gpost_ref, gpre_ref, h_ref, *, n_first):
    i = pl.program_id(0)

    def body(x_ref):
        x1 = x_ref[...] + _rms(mix_ref[...].astype(jnp.float32), gpost_ref[...])
        h_ref[...] = _rms(x1, gpre_ref[...]).astype(h_ref.dtype)

    @pl.when(i < n_first)
    def _():
        body(xp_ref)

    @pl.when(i >= n_first)
    def _():
        body(xs_ref)


def _post1(xp, xs, mix, gpost, gpre, tb=256):
    n_first, n_second = xp.shape[0] // tb, xs.shape[0] // tb
    d = xp.shape[1]
    n = mix.shape[0]
    row = pl.BlockSpec((tb, d), lambda i: (i, 0))
    vec = pl.BlockSpec((1, d), lambda i: (0, 0))
    return pl.pallas_call(
        functools.partial(_post1_kernel, n_first=n_first),
        grid=(n_first + n_second,),
        in_specs=[
            pl.BlockSpec((tb, d), lambda i: (jnp.minimum(i, n_first - 1), 0)),
            pl.BlockSpec((tb, d), lambda i: (jnp.maximum(i - n_first, 0), 0)),
            row, vec, vec,
        ],
        out_specs=row,
        out_shape=jax.ShapeDtypeStruct((n, d), jnp.bfloat16),
        compiler_params=_cparams(("arbitrary",)),
        name="post_mix",
    )(xp, xs, mix, gpost, gpre)


def _post2_kernel(x_ref, mix_ref, f_ref, gmix_ref, gffn_ref, o_ref):
    x1 = x_ref[...] + _rms(mix_ref[...].astype(jnp.float32), gmix_ref[...])
    o_ref[...] = x1 + _rms(f_ref[...].astype(jnp.float32), gffn_ref[...])


def _post2(x, mix, f, gmix, gffn, row0, tb=256):
    nrows, d = x.shape
    b0 = row0 // tb
    own = pl.BlockSpec((tb, d), lambda i: (i, 0))
    src = pl.BlockSpec((tb, d), lambda i: (i + b0, 0))
    vec = pl.BlockSpec((1, d), lambda i: (0, 0))
    return pl.pallas_call(
        _post2_kernel,
        grid=(nrows // tb,),
        in_specs=[own, src, src, vec, vec],
        out_specs=own,
        out_shape=jax.ShapeDtypeStruct((nrows, d), jnp.float32),
        compiler_params=_cparams(("parallel",)),
        name="post_ffn",
    )(x, mix, f, gmix, gffn)


def _na_bias_kernel(rpb_ref, o_ref, *, patterns):
    h = pl.program_id(0)
    n_ic = 2 * NA_COLS - 1
    n_ir = 2 * NA_MAX_ROWS - 1
    shape = (GRID_W, LANES)
    qc = lax.broadcasted_iota(jnp.int32, shape, 0)
    lane = lax.broadcasted_iota(jnp.int32, shape, 1)
    kc = lane & (GRID_W - 1)
    delta = kc - qc + (NA_COLS - 1)
    cq = jnp.clip(qc - NA_COLS // 2, 0, GRID_W - NA_COLS)
    col_ok = (kc >= cq) & (kc < cq + NA_COLS)
    left = lane < GRID_W
    neg = jnp.full(shape, NEG_INF, jnp.float32)

    w = []
    for e in range(n_ir):
        acc = jnp.zeros(shape, jnp.float32)
        for d in range(n_ic):
            acc = jnp.where(delta == d, rpb_ref[h * (n_ir * n_ic) + e * n_ic + d], acc)
        w.append(acc * LOG2E)
    zero = jnp.zeros(shape, jnp.float32)

    def tile(dr):
        e = dr + NA_MAX_ROWS - 1
        return w[e] if 0 <= e < n_ir else zero

    for p, (rel, win) in enumerate(patterns):
        for a in range(NA_STEP_ROWS):
            for c in range(NA_KEY_BLOCKS):
                b0 = 2 * c
                dr = b0 - a - rel
                ok0 = win[a] <= b0 < win[a] + NA_MAX_ROWS
                ok1 = win[a] <= b0 + 1 < win[a] + NA_MAX_ROWS
                pair = jnp.where(left, tile(dr), tile(dr + 1))
                if ok0 and ok1:
                    ok = col_ok
                elif ok0:
                    ok = col_ok & left
                elif ok1:
                    ok = col_ok & jnp.logical_not(left)
                else:
                    ok = None
                val = neg if ok is None else jnp.where(ok, pair, neg)
                if c == NA_KEY_BLOCKS - 1:
                    assert not ok1
                    val = jnp.where(lane >= LANES - N_META, 0.0, val)
                o_ref[p, 0, a * GRID_W:(a + 1) * GRID_W, c * LANES:(c + 1) * LANES] = val


def _na_bias(rpb_flat, patterns):
    npat = len(patterns)
    q = NA_STEP_BLOCKS * TOK_BLOCK
    return pl.pallas_call(
        functools.partial(_na_bias_kernel, patterns=tuple(patterns)),
        grid=(N_NA_HEADS,),
        in_specs=[pl.BlockSpec(memory_space=pltpu.SMEM)],
        out_specs=pl.BlockSpec((npat, 1, q, NA_KEYS), lambda h: (0, h, 0, 0)),
        out_shape=jax.ShapeDtypeStruct((npat, N_NA_HEADS, q, NA_KEYS), jnp.float32),
        compiler_params=_cparams(("parallel",)),
        name="na_bias",
    )(rpb_flat)


_NT = (((1,), (1,)), ((), ()))


def _group_norm_store(oacc_ref, ss, g_ref, o_ref):
    inv = lax.rsqrt(ss * (1.0 / oacc_ref.shape[1]) + EPS)
    o_ref[...] = (oacc_ref[...] * inv * g_ref[...]).astype(o_ref.dtype)


def _na_kernel(kb_ref, pat_ref, q_ref, *rest):
    k_refs = rest[:NA_KEY_BLOCKS]
    v_refs = rest[NA_KEY_BLOCKS:2 * NA_KEY_BLOCKS]
    km_ref, vm_ref, bias_ref, g_ref, o_ref, oacc_ref, sa_ref, sb_ref = rest[2 * NA_KEY_BLOCKS:]
    nq = q_ref.shape[0]
    real = TOK_BLOCK - BF16_SUBLANES

    def with_meta(refs, m_ref, sl):
        return jnp.concatenate([r[:, sl] for r in refs[:-1]] + [refs[-1][:real, sl], m_ref[real:, sl]], axis=0)

    def logits(h, s_ref):
        sl = slice(h * HEAD_DIM, (h + 1) * HEAD_DIM)
        k = with_meta(k_refs, km_ref, sl)
        s_ref[...] = lax.dot_general(q_ref[:, sl], k, _NT, preferred_element_type=jnp.float32) + bias_ref[0, h]

    def softmax_pv(h, s_ref, ss):
        sl = slice(h * HEAD_DIM, (h + 1) * HEAD_DIM)
        s = s_ref[...]
        m = jnp.max(s, axis=-1, keepdims=True)
        p = jnp.exp2(s - m)
        l = jnp.sum(p, axis=-1, keepdims=True)
        v = with_meta(v_refs, vm_ref, sl)
        o = jnp.dot(p.astype(v.dtype), v, preferred_element_type=jnp.float32) / l
        oacc_ref[:, sl] = o
        return ss + jnp.sum(o * o, axis=-1, keepdims=True)

    bufs = (sa_ref, sb_ref)
    ss = jnp.zeros((nq, 1), jnp.float32)
    logits(0, bufs[0])
    for h in range(N_NA_HEADS):
        if h + 1 < N_NA_HEADS:
            logits(h + 1, bufs[(h + 1) % 2])
        ss = softmax_pv(h, bufs[h % 2], ss)
    _group_norm_store(oacc_ref, ss, g_ref, o_ref)


def _na_attention(proj, projm, bias, g, tables):
    n = proj.shape[0]
    nblk = n // TOK_BLOCK
    nq = NA_STEP_BLOCKS * TOK_BLOCK
    qblk = (nq, D_NA)
    kblk = (TOK_BLOCK, D_NA)

    def kv_spec(col, j):
        return pl.BlockSpec(kblk, lambda i, kb, pat: (jnp.minimum(kb[i] + j, nblk - 1), col))

    in_specs = [pl.BlockSpec(qblk, lambda i, kb, pat: (i, 0))]
    in_specs += [kv_spec(1, j) for j in range(NA_KEY_BLOCKS)]
    in_specs += [kv_spec(2, j) for j in range(NA_KEY_BLOCKS)]
    in_specs += [
        pl.BlockSpec(kblk, lambda i, kb, pat: (0, 0)),
        pl.BlockSpec(kblk, lambda i, kb, pat: (0, 1)),
        pl.BlockSpec((1, N_NA_HEADS, nq, NA_KEYS), lambda i, kb, pat: (pat[i], 0, 0, 0)),
        pl.BlockSpec((1, D_NA), lambda i, kb, pat: (0, 0)),
    ]
    return pl.pallas_call(
        _na_kernel,
        grid_spec=pltpu.PrefetchScalarGridSpec(
            num_scalar_prefetch=2,
            grid=(n // nq,),
            in_specs=in_specs,
            out_specs=pl.BlockSpec(qblk, lambda i, kb, pat: (i, 0)),
            scratch_shapes=[pltpu.VMEM(qblk, jnp.float32),
                            pltpu.VMEM((nq, NA_KEYS), jnp.float32),
                            pltpu.VMEM((nq, NA_KEYS), jnp.float32)],
        ),
        out_shape=jax.ShapeDtypeStruct((n, D_NA), jnp.bfloat16),
        compiler_params=_cparams(("arbitrary",)),
        name="na_attention",
    )(tables["kb"], tables["pat"], proj, *([proj] * (2 * NA_KEY_BLOCKS)), projm, projm, bias, g)


def _swa_bias_kernel(slope_ref, o_ref):
    slope = slope_ref[pl.program_id(0)]
    shape = (SW_BLOCK, SW_KEYS)
    r = lax.broadcasted_iota(jnp.int32, shape, 0)
    c = lax.broadcasted_iota(jnp.int32, shape, 1)
    n_real = 3 * SW_BLOCK
    rel = c - r
    band = (rel >= 0) & (rel <= 2 * SW_WINDOW)
    is_real = c < n_real
    j = c - n_real
    is_meta = (c >= n_real) & (j < N_META)
    for p in range(4):
        first, last = p & 1, p >> 1
        c_lo = SW_BLOCK if first else 0
        c_hi = 2 * SW_BLOCK if last else 3 * SW_BLOCK
        valid = (band & (c >= c_lo) & (c < c_hi)) | is_meta
        mdist = jnp.minimum(N_META + r - j, SW_WINDOW) if first else jnp.full(shape, SW_WINDOW, jnp.int32)
        dist = jnp.where(is_real, jnp.abs(rel - SW_BLOCK), mdist).astype(jnp.float32)
        o_ref[p, 0] = jnp.where(valid, -slope * dist, NEG_INF)


def _swa_bias(slopes):
    return pl.pallas_call(
        _swa_bias_kernel,
        grid=(N_SW_HEADS,),
        in_specs=[pl.BlockSpec(memory_space=pltpu.SMEM)],
        out_specs=pl.BlockSpec((4, 1, SW_BLOCK, SW_KEYS), lambda h: (0, h, 0, 0)),
        out_shape=jax.ShapeDtypeStruct((4, N_SW_HEADS, SW_BLOCK, SW_KEYS), jnp.float32),
        compiler_params=_cparams(("parallel",)),
        name="swa_bias",
    )(slopes)


def _swa_kernel(prev_ref, nxt_ref, pat_ref, sink_ref, q_ref, kp_ref, kc_ref, kn_ref,
                vp_ref, vc_ref, vn_ref, km_ref, vm_ref, bias_ref, g_ref, o_ref, oacc_ref, sa_ref, sb_ref):
    def group_heads(kh):
        return [kh * SW_GROUP + gi for gi in range(SW_GROUP)]

    def logits(kh, s_ref):
        ksl = slice(kh * HEAD_DIM, (kh + 1) * HEAD_DIM)
        k = jnp.concatenate([kp_ref[:, ksl], kc_ref[:, ksl], kn_ref[:, ksl], km_ref[:, ksl]], axis=0)
        q = jnp.concatenate([q_ref[:, h * HEAD_DIM:(h + 1) * HEAD_DIM] for h in group_heads(kh)], axis=0)
        s_all = lax.dot_general(q, k, _NT, preferred_element_type=jnp.float32)
        for gi, h in enumerate(group_heads(kh)):
            rows = slice(gi * SW_BLOCK, (gi + 1) * SW_BLOCK)
            s_ref[rows] = s_all[rows] + bias_ref[0, h]

    def softmax_pv(kh, s_ref, ss):
        ksl = slice(kh * HEAD_DIM, (kh + 1) * HEAD_DIM)
        v = jnp.concatenate([vp_ref[:, ksl], vc_ref[:, ksl], vn_ref[:, ksl], vm_ref[:, ksl]], axis=0)
        ps, ls = [], []
        for gi, h in enumerate(group_heads(kh)):
            s = s_ref[gi * SW_BLOCK:(gi + 1) * SW_BLOCK]
            sink = sink_ref[h] * LOG2E
            m = jnp.maximum(jnp.max(s, axis=-1, keepdims=True), sink)
            p = jnp.exp2(s - m)
            ls.append(jnp.sum(p, axis=-1, keepdims=True) + jnp.exp2(sink - m))
            ps.append(p.astype(v.dtype))
        o_all = jnp.dot(jnp.concatenate(ps, axis=0), v, preferred_element_type=jnp.float32)
        for gi, h in enumerate(group_heads(kh)):
            o = o_all[gi * SW_BLOCK:(gi + 1) * SW_BLOCK] / ls[gi]
            ss = ss + jnp.sum(o * o, axis=-1, keepdims=True)
            oacc_ref[:, h * HEAD_DIM:(h + 1) * HEAD_DIM] = o
        return ss

    bufs = (sa_ref, sb_ref)
    ss = jnp.zeros((SW_BLOCK, 1), jnp.float32)
    logits(0, bufs[0])
    for kh in range(N_SW_KV_HEADS):
        if kh + 1 < N_SW_KV_HEADS:
            logits(kh + 1, bufs[(kh + 1) % 2])
        ss = softmax_pv(kh, bufs[kh % 2], ss)
    _group_norm_store(oacc_ref, ss, g_ref, o_ref)


def _swa_attention(proj, projm, sink, bias, g, tables):
    n = proj.shape[0]
    nblk = n // SW_BLOCK
    qcol = 3 * D_NA // D_SW
    kcol = (3 * D_NA + D_SW) // D_SW_KV
    vcol = kcol + 1
    qblk = (SW_BLOCK, D_SW)
    kvblk = (SW_BLOCK, D_SW_KV)

    def kv(col, which):
        if which == "prev":
            return pl.BlockSpec(kvblk, lambda i, pv, nx, pt: (pv[i], col))
        if which == "cur":
            return pl.BlockSpec(kvblk, lambda i, pv, nx, pt: (i, col))
        if which == "next":
            return pl.BlockSpec(kvblk, lambda i, pv, nx, pt: (nx[i], col))
        return pl.BlockSpec(kvblk, lambda i, pv, nx, pt: (0, col))

    in_specs = [pl.BlockSpec(memory_space=pltpu.SMEM),
                pl.BlockSpec(qblk, lambda i, pv, nx, pt: (i, qcol))]
    in_specs += [kv(kcol, w) for w in ("prev", "cur", "next")]
    in_specs += [kv(vcol, w) for w in ("prev", "cur", "next")]
    mcol = 2 * D_NA // D_SW_KV
    in_specs += [kv(mcol, "meta"), kv(mcol + 1, "meta"),
                 pl.BlockSpec((1, N_SW_HEADS, SW_BLOCK, SW_KEYS), lambda i, pv, nx, pt: (pt[i], 0, 0, 0)),
                 pl.BlockSpec((1, D_SW), lambda i, pv, nx, pt: (0, 0))]
    return pl.pallas_call(
        _swa_kernel,
        grid_spec=pltpu.PrefetchScalarGridSpec(
            num_scalar_prefetch=3,
            grid=(nblk,),
            in_specs=in_specs,
            out_specs=pl.BlockSpec(qblk, lambda i, pv, nx, pt: (i, 0)),
            scratch_shapes=[pltpu.VMEM(qblk, jnp.float32),
                            pltpu.VMEM((SW_GROUP * SW_BLOCK, SW_KEYS), jnp.float32),
                            pltpu.VMEM((SW_GROUP * SW_BLOCK, SW_KEYS), jnp.float32)],
        ),
        out_shape=jax.ShapeDtypeStruct((n, D_SW), jnp.bfloat16),
        compiler_params=_cparams(("arbitrary",)),
        name="swa_attention",
    )(tables["prev"], tables["nxt"], tables["swpat"], sink, proj,
      proj, proj, proj, proj, proj, proj, projm, projm, bias, g)


def kernel(x_prompt, x_sample, meta_tokens, w_in, w_out, rpb, sink, g_na_out, g_sw_out,
           g_pre_mix, g_post_mix, g_pre_ffn, g_post_ffn, w_gate, w_up, w_down):
    xp = x_prompt.reshape(-1, D_MODEL)
    xs = x_sample.reshape(-1, D_MODEL)
    n_p = xp.shape[0]
    seq_blocks = [x_prompt.shape[1] // TOK_BLOCK] * x_prompt.shape[0]
    seq_blocks += [x_sample.shape[1] // TOK_BLOCK] * x_sample.shape[0]
    tables, patterns = _block_tables(seq_blocks)
    row = lambda v: v.reshape(1, -1)

    scale = HEAD_DIM ** -0.5 * LOG2E
    colscale = np.ones((1, D_IN), np.float32)
    colscale[:, :D_NA] = scale
    colscale[:, 3 * D_NA:3 * D_NA + D_SW] = scale
    colscale = jnp.asarray(colscale)

    h = _prenorm2(xp, xs, row(g_pre_mix[0]))
    proj = _mm_in(h, w_in[0], colscale, bm=1024)
    gap = jnp.zeros((TOK_BLOCK - 2 * N_META, D_MODEL), meta_tokens.dtype)
    meta_blk = jnp.concatenate([meta_tokens, gap, meta_tokens], axis=0)
    hm = _prenorm1(meta_blk, row(g_pre_mix[0]))
    bn = 2 * D_SW_KV
    n_kv, kv0, sw_kv = 2 * D_NA // bn, D_NA // bn, (3 * D_NA + D_SW) // bn
    projm = _mm_in(hm, w_in[0], colscale, bm=TOK_BLOCK, bn=bn, name="mm_in_meta", col_steps=n_kv + 1,
                   col_of=lambda j: jnp.where(j < n_kv, j + kv0, sw_kv))

    bias = _na_bias(rpb[0].reshape(-1), patterns)
    o_na = _na_attention(proj, projm, bias, row(g_na_out[0]), tables)
    slopes = np.asarray([2.0 ** (-8.0 * (i + 1) / N_SW_HEADS) * LOG2E for i in range(N_SW_HEADS)], np.float32)
    o_sw = _swa_attention(proj, projm, sink[0], _swa_bias(jnp.asarray(slopes)), row(g_sw_out[0]), tables)

    mix = _mm_out(o_na, o_sw, w_out[0])
    h2 = _post1(xp, xs, mix, row(g_post_mix[0]), row(g_pre_ffn[0]))

    act_main = _mm_gateup(h2, w_gate[0], w_up[0], 0, FF_MAIN, 512, "mm_gateup")
    act_tail = _mm_gateup(h2, w_gate[0], w_up[0], FF_MAIN, FF_TAIL, FF_TAIL, "mm_gateup_tail")
    f = _mm_down(act_main, act_tail, w_down[0])
    y_p = _post2(xp, mix, f, row(g_post_mix[0]), row(g_post_ffn[0]), 0)
    y_s = _post2(xs, mix, f, row(g_post_mix[0]), row(g_post_ffn[0]), n_p)
    return (y_p.reshape(x_prompt.shape), y_s.reshape(x_sample.shape))
```

```python
import functools
import math

import numpy as np
import jax
import jax.numpy as jnp
from jax import lax
from jax.experimental import pallas as pl
from jax.experimental.pallas import tpu as pltpu

D_MODEL = 4096
DEPTH = 1
HEAD_DIM = 128
N_HEADS = D_MODEL // HEAD_DIM
N_NA_HEADS = N_HEADS // 2
N_SW_HEADS = N_HEADS - N_NA_HEADS
N_SW_KV_HEADS = max(1, N_SW_HEADS // 4)
SW_GROUP = N_SW_HEADS // N_SW_KV_HEADS
D_NA = N_NA_HEADS * HEAD_DIM
D_SW = N_SW_HEADS * HEAD_DIM
D_SW_KV = N_SW_KV_HEADS * HEAD_DIM
D_IN = 3 * D_NA + D_SW + 2 * D_SW_KV
D_FF = -(-8 * D_MODEL // (3 * 256)) * 256
GRID_W = 64
NA_MAX_ROWS = 8
NA_COLS = 16
NA_QROWS = 2
SW_WINDOW = 128
SW_BLOCK = 128
N_META = 16
EPS = 1e-6
NEG_INF = -1e30

LANES = 128
BF16_SUBLANES = 16
TOK_BLOCK = NA_QROWS * GRID_W
NA_STEP_BLOCKS = 2
NA_STEP_ROWS = NA_STEP_BLOCKS * NA_QROWS
NA_KEY_ROWS = NA_STEP_ROWS + NA_MAX_ROWS
NA_KEY_BLOCKS = NA_KEY_ROWS * GRID_W // TOK_BLOCK
NA_KEYS = NA_KEY_BLOCKS * TOK_BLOCK
SW_KEYS = 4 * SW_BLOCK
MM_TILE = 1024
GATEUP_BN = 512
FF_MAIN = D_FF // GATEUP_BN * GATEUP_BN
FF_TAIL = D_FF - FF_MAIN
PRENORM_ROWS = 512
POST_ROWS = 256
VMEM_LIMIT = 56 * 1024 * 1024
VMEM_SLACK = 2 * 1024 * 1024
LOG2E = math.log2(math.e)

assert SW_BLOCK == TOK_BLOCK and DEPTH == 1 and FF_TAIL == 256


def _cparams(sem, vmem=VMEM_LIMIT):
    return pltpu.CompilerParams(dimension_semantics=sem, vmem_limit_bytes=vmem)


def _ew_vmem(tb, d, elem_bytes):
    return 2 * tb * d * sum(elem_bytes) + 2 * tb * d * 4 + VMEM_SLACK


def _block_tables(seq_blocks):
    kb, pat, prev, nxt, swpat = [], [], [], [], []
    patterns = []
    start_blk = 0
    for nb in seq_blocks:
        rows = nb * NA_QROWS
        assert nb % NA_STEP_BLOCKS == 0 and rows >= NA_KEY_ROWS
        for st in range(nb // NA_STEP_BLOCKS):
            r0 = st * NA_STEP_ROWS
            ks = max(r0 - NA_MAX_ROWS // 2, 0)
            assert ks % NA_QROWS == 0
            w = tuple(int(np.clip(r0 + a - NA_MAX_ROWS // 2, 0, rows - NA_MAX_ROWS)) - ks
                      for a in range(NA_STEP_ROWS))
            assert all(0 <= wa and wa + NA_MAX_ROWS <= NA_KEY_ROWS - 1 for wa in w)
            key = (r0 - ks, w)
            if key not in patterns:
                patterns.append(key)
            kb.append(start_blk + ks // NA_QROWS)
            pat.append(patterns.index(key))
        for bl in range(nb):
            prev.append(start_blk + max(bl - 1, 0))
            nxt.append(start_blk + min(bl + 1, nb - 1))
            swpat.append(int(bl == 0) + 2 * int(bl == nb - 1))
        start_blk += nb
    i32 = lambda v: jnp.asarray(np.asarray(v, np.int32))
    return dict(kb=i32(kb), pat=i32(pat), prev=i32(prev), nxt=i32(nxt), swpat=i32(swpat)), patterns


def _rms(x, g):
    return x * lax.rsqrt(jnp.mean(x * x, axis=-1, keepdims=True) + EPS) * g


def _prenorm2_kernel(xp_ref, xs_ref, g_ref, o_ref, *, n_first):
    i = pl.program_id(0)

    @pl.when(i < n_first)
    def _():
        o_ref[...] = _rms(xp_ref[...], g_ref[...]).astype(o_ref.dtype)

    @pl.when(i >= n_first)
    def _():
        o_ref[...] = _rms(xs_ref[...], g_ref[...]).astype(o_ref.dtype)


def _prenorm2(xp, xs, g, tb=PRENORM_ROWS):
    n_first, n_second = xp.shape[0] // tb, xs.shape[0] // tb
    d = xp.shape[1]
    return pl.pallas_call(
        functools.partial(_prenorm2_kernel, n_first=n_first),
        grid=(n_first + n_second,),
        in_specs=[
            pl.BlockSpec((tb, d), lambda i: (jnp.minimum(i, n_first - 1), 0)),
            pl.BlockSpec((tb, d), lambda i: (jnp.maximum(i - n_first, 0), 0)),
            pl.BlockSpec((1, d), lambda i: (0, 0)),
        ],
        out_specs=pl.BlockSpec((tb, d), lambda i: (i, 0)),
        out_shape=jax.ShapeDtypeStruct((xp.shape[0] + xs.shape[0], d), jnp.bfloat16),
        compiler_params=_cparams(("arbitrary",), _ew_vmem(tb, d, (4, 4, 2))),
        name="prenorm",
    )(xp, xs, g)


def _prenorm1_kernel(x_ref, g_ref, o_ref):
    o_ref[...] = _rms(x_ref[...], g_ref[...]).astype(o_ref.dtype)


def _prenorm1(x, g):
    n, d = x.shape
    return pl.pallas_call(
        _prenorm1_kernel,
        grid=(1,),
        in_specs=[pl.BlockSpec((n, d), lambda i: (0, 0)), pl.BlockSpec((1, d), lambda i: (0, 0))],
        out_specs=pl.BlockSpec((n, d), lambda i: (0, 0)),
        out_shape=jax.ShapeDtypeStruct((n, d), jnp.bfloat16),
        compiler_params=_cparams(("arbitrary",)),
        name="prenorm_meta",
    )(x, g)


def _mm_in_kernel(a_ref, b_ref, s_ref, o_ref):
    acc = jnp.dot(a_ref[...], b_ref[...].astype(jnp.bfloat16), preferred_element_type=jnp.float32)
    o_ref[...] = (acc * s_ref[...]).astype(o_ref.dtype)


def _mm_in(a, b, colscale, bm, bn=MM_TILE, name="mm_in", col_steps=None, col_of=lambda j: j):
    m, k = a.shape
    col_steps = b.shape[1] // bn if col_steps is None else col_steps
    return pl.pallas_call(
        _mm_in_kernel,
        grid=(m // bm, col_steps),
        in_specs=[
            pl.BlockSpec((bm, k), lambda i, j: (i, 0)),
            pl.BlockSpec((k, bn), lambda i, j: (0, col_of(j))),
            pl.BlockSpec((1, bn), lambda i, j: (0, col_of(j))),
        ],
        out_specs=pl.BlockSpec((bm, bn), lambda i, j: (i, j)),
        out_shape=jax.ShapeDtypeStruct((m, col_steps * bn), jnp.bfloat16),
        compiler_params=_cparams(("parallel", "parallel"),
                                 2 * bm * k * 2 + 2 * k * bn * 4 + k * bn * 2 + 2 * bm * bn * 2 + VMEM_SLACK),
        name=name,
    )(a, b, colscale)


def _mm_out_kernel(a1_ref, a2_ref, b1_ref, b2_ref, o_ref):
    acc = jnp.dot(a1_ref[...], b1_ref[...].astype(jnp.bfloat16), preferred_element_type=jnp.float32)
    acc = acc + jnp.dot(a2_ref[...], b2_ref[...].astype(jnp.bfloat16), preferred_element_type=jnp.float32)
    o_ref[...] = acc.astype(o_ref.dtype)


def _mm_out(a1, a2, b, bm=MM_TILE, bn=MM_TILE):
    m, k1 = a1.shape
    k2 = a2.shape[1]
    n = b.shape[1]
    assert k1 == k2 and b.shape[0] == k1 + k2
    vmem = 4 * bm * k1 * 2 + 4 * k1 * bn * 4 + 2 * k1 * bn * 2 + 2 * bm * bn * 2 + VMEM_SLACK
    return pl.pallas_call(
        _mm_out_kernel,
        grid=(m // bm, n // bn),
        in_specs=[
            pl.BlockSpec((bm, k1), lambda i, j: (i, 0)),
            pl.BlockSpec((bm, k2), lambda i, j: (i, 0)),
            pl.BlockSpec((k1, bn), lambda i, j: (0, j)),
            pl.BlockSpec((k2, bn), lambda i, j: (1, j)),
        ],
        out_specs=pl.BlockSpec((bm, bn), lambda i, j: (i, j)),
        out_shape=jax.ShapeDtypeStruct((m, n), jnp.bfloat16),
        compiler_params=_cparams(("parallel", "parallel"), vmem),
        name="mm_out",
    )(a1, a2, b, b)


def _mm_gateup_kernel(a_ref, bg_ref, bu_ref, o_ref):
    a = a_ref[...]
    g = jnp.dot(a, bg_ref[...].astype(jnp.bfloat16), preferred_element_type=jnp.float32)
    u = jnp.dot(a, bu_ref[...].astype(jnp.bfloat16), preferred_element_type=jnp.float32)
    o_ref[...] = (g / (1.0 + jnp.exp(-g)) * u).astype(o_ref.dtype)


def _mm_gateup(a, bg, bu, col0, ncols, bn, name, bm=MM_TILE):
    m, k = a.shape
    assert col0 % bn == 0 and ncols % bn == 0
    j0 = col0 // bn
    wspec = pl.BlockSpec((k, bn), lambda i, j: (0, j + j0))
    return pl.pallas_call(
        _mm_gateup_kernel,
        grid=(m // bm, ncols // bn),
        in_specs=[pl.BlockSpec((bm, k), lambda i, j: (i, 0)), wspec, wspec],
        out_specs=pl.BlockSpec((bm, bn), lambda i, j: (i, j)),
        out_shape=jax.ShapeDtypeStruct((m, ncols), jnp.bfloat16),
        compiler_params=_cparams(("parallel", "parallel"),
                                 2 * bm * k * 2 + 4 * k * bn * 4 + 2 * k * bn * 2 + 2 * bm * bn * 2 + VMEM_SLACK),
        name=name,
    )(a, bg, bu)


def _mm_down_kernel(a1_ref, a2_ref, b1_ref, b2_ref, o_ref, acc_ref):
    kk = pl.program_id(2)

    @pl.when(kk == 0)
    def _():
        acc_ref[...] = jnp.dot(a2_ref[...], b2_ref[...].astype(jnp.bfloat16), preferred_element_type=jnp.float32)

    acc_ref[...] += jnp.dot(a1_ref[...], b1_ref[...].astype(jnp.bfloat16), preferred_element_type=jnp.float32)

    @pl.when(kk == pl.num_programs(2) - 1)
    def _():
        o_ref[...] = acc_ref[...].astype(o_ref.dtype)


def _mm_down(a1, a2, b, bm=MM_TILE, bn=MM_TILE, ksteps=3):
    m, k1 = a1.shape
    k2 = a2.shape[1]
    n = b.shape[1]
    bk = k1 // ksteps
    assert b.shape[0] == k1 + k2 and k1 % k2 == 0 and k1 % ksteps == 0 and bk % LANES == 0
    vmem = (2 * bm * (bk + k2) * 2 + 2 * (bk + k2) * bn * 4 + (bk + k2) * bn * 2 + 2 * bm * bn * 2
            + bm * bn * 4 + VMEM_SLACK)
    return pl.pallas_call(
        _mm_down_kernel,
        grid=(m // bm, n // bn, ksteps),
        in_specs=[
            pl.BlockSpec((bm, bk), lambda i, j, kk: (i, kk)),
            pl.BlockSpec((bm, k2), lambda i, j, kk: (i, 0)),
            pl.BlockSpec((bk, bn), lambda i, j, kk: (kk, j)),
            pl.BlockSpec((k2, bn), lambda i, j, kk: (k1 // k2, j)),
        ],
        out_specs=pl.BlockSpec((bm, bn), lambda i, j, kk: (i, j)),
        out_shape=jax.ShapeDtypeStruct((m, n), jnp.bfloat16),
        scratch_shapes=[pltpu.VMEM((bm, bn), jnp.float32)],
        compiler_params=_cparams(("parallel", "parallel", "arbitrary"), vmem),
        name="mm_down",
    )(a1, a2, b, b)


def _post1_kernel(xp_ref, xs_ref, mix_ref, gpost_ref, gpre_ref, h_ref, *, n_first):
    i = pl.program_id(0)

    def body(x_ref):
        x1 = x_ref[...] + _rms(mix_ref[...].astype(jnp.float32), gpost_ref[...])
        h_ref[...] = _rms(x1, gpre_ref[...]).astype(h_ref.dtype)

    @pl.when(i < n_first)
    def _():
        body(xp_ref)

    @pl.when(i >= n_first)
    def _():
        body(xs_ref)


def _post1(xp, xs, mix, gpost, gpre, tb=POST_ROWS):
    n_first, n_second = xp.shape[0] // tb, xs.shape[0] // tb
    d = xp.shape[1]
    n = mix.shape[0]
    row = pl.BlockSpec((tb, d), lambda i: (i, 0))
    vec = pl.BlockSpec((1, d), lambda i: (0, 0))
    return pl.pallas_call(
        functools.partial(_post1_kernel, n_first=n_first),
        grid=(n_first + n_second,),
        in_specs=[
            pl.BlockSpec((tb, d), lambda i: (jnp.minimum(i, n_first - 1), 0)),
            pl.BlockSpec((tb, d), lambda i: (jnp.maximum(i - n_first, 0), 0)),
            row, vec, vec,
        ],
        out_specs=row,
        out_shape=jax.ShapeDtypeStruct((n, d), jnp.bfloat16),
        compiler_params=_cparams(("arbitrary",), _ew_vmem(tb, d, (4, 4, 2, 2))),
        name="post_mix",
    )(xp, xs, mix, gpost, gpre)


def _post2_kernel(x_ref, mix_ref, f_ref, gmix_ref, gffn_ref, o_ref):
    x1 = x_ref[...] + _rms(mix_ref[...].astype(jnp.float32), gmix_ref[...])
    o_ref[...] = x1 + _rms(f_ref[...].astype(jnp.float32), gffn_ref[...])


def _post2(x, mix, f, gmix, gffn, row0, tb=POST_ROWS):
    nrows, d = x.shape
    b0 = row0 // tb
    own = pl.BlockSpec((tb, d), lambda i: (i, 0))
    src = pl.BlockSpec((tb, d), lambda i: (i + b0, 0))
    vec = pl.BlockSpec((1, d), lambda i: (0, 0))
    return pl.pallas_call(
        _post2_kernel,
        grid=(nrows // tb,),
        in_specs=[own, src, src, vec, vec],
        out_specs=own,
        out_shape=jax.ShapeDtypeStruct((nrows, d), jnp.float32),
        compiler_params=_cparams(("parallel",), _ew_vmem(tb, d, (4, 2, 2, 4))),
        name="post_ffn",
    )(x, mix, f, gmix, gffn)


def _na_bias_kernel(rpb_ref, o_ref, *, patterns):
    h = pl.program_id(0)
    n_ic = 2 * NA_COLS - 1
    n_ir = 2 * NA_MAX_ROWS - 1
    shape = (GRID_W, LANES)
    qc = lax.broadcasted_iota(jnp.int32, shape, 0)
    lane = lax.broadcasted_iota(jnp.int32, shape, 1)
    kc = lane & (GRID_W - 1)
    delta = kc - qc + (NA_COLS - 1)
    cq = jnp.clip(qc - NA_COLS // 2, 0, GRID_W - NA_COLS)
    col_ok = (kc >= cq) & (kc < cq + NA_COLS)
    left = lane < GRID_W
    neg = jnp.full(shape, NEG_INF, jnp.float32)

    w = []
    for e in range(n_ir):
        acc = jnp.zeros(shape, jnp.float32)
        for d in range(n_ic):
            acc = jnp.where(delta == d, rpb_ref[h * (n_ir * n_ic) + e * n_ic + d], acc)
        w.append(acc * LOG2E)
    zero = jnp.zeros(shape, jnp.float32)

    def tile(dr):
        e = dr + NA_MAX_ROWS - 1
        return w[e] if 0 <= e < n_ir else zero

    for p, (rel, win) in enumerate(patterns):
        for a in range(NA_STEP_ROWS):
            for c in range(NA_KEY_BLOCKS):
                b0 = 2 * c
                dr = b0 - a - rel
                ok0 = win[a] <= b0 < win[a] + NA_MAX_ROWS
                ok1 = win[a] <= b0 + 1 < win[a] + NA_MAX_ROWS
                pair = jnp.where(left, tile(dr), tile(dr + 1))
                if ok0 and ok1:
                    ok = col_ok
                elif ok0:
                    ok = col_ok & left
                elif ok1:
                    ok = col_ok & jnp.logical_not(left)
                else:
                    ok = None
                val = neg if ok is None else jnp.where(ok, pair, neg)
                if c == NA_KEY_BLOCKS - 1:
                    assert not ok1
                    val = jnp.where(lane >= LANES - N_META, 0.0, val)
                o_ref[p, 0, a * GRID_W:(a + 1) * GRID_W, c * LANES:(c + 1) * LANES] = val


def _na_bias(rpb_flat, patterns):
    npat = len(patterns)
    q = NA_STEP_BLOCKS * TOK_BLOCK
    return pl.pallas_call(
        functools.partial(_na_bias_kernel, patterns=tuple(patterns)),
        grid=(N_NA_HEADS,),
        in_specs=[pl.BlockSpec(memory_space=pltpu.SMEM)],
        out_specs=pl.BlockSpec((npat, 1, q, NA_KEYS), lambda h: (0, h, 0, 0)),
        out_shape=jax.ShapeDtypeStruct((npat, N_NA_HEADS, q, NA_KEYS), jnp.float32),
        compiler_params=_cparams(("parallel",)),
        name="na_bias",
    )(rpb_flat)


_NT = (((1,), (1,)), ((), ()))


def _group_norm_store(oacc_ref, ss, g_ref, o_ref):
    inv = lax.rsqrt(ss * (1.0 / oacc_ref.shape[1]) + EPS)
    o_ref[...] = (oacc_ref[...] * inv * g_ref[...]).astype(o_ref.dtype)


def _na_kernel(kb_ref, pat_ref, q_ref, *rest):
    k_refs = rest[:NA_KEY_BLOCKS]
    v_refs = rest[NA_KEY_BLOCKS:2 * NA_KEY_BLOCKS]
    km_ref, vm_ref, bias_ref, g_ref, o_ref, oacc_ref, sa_ref, sb_ref = rest[2 * NA_KEY_BLOCKS:]
    nq = q_ref.shape[0]
    real = TOK_BLOCK - BF16_SUBLANES

    def with_meta(refs, m_ref, sl):
        return jnp.concatenate([r[:, sl] for r in refs[:-1]] + [refs[-1][:real, sl], m_ref[real:, sl]], axis=0)

    def logits(h, s_ref):
        sl = slice(h * HEAD_DIM, (h + 1) * HEAD_DIM)
        k = with_meta(k_refs, km_ref, sl)
        s_ref[...] = lax.dot_general(q_ref[:, sl], k, _NT, preferred_element_type=jnp.float32) + bias_ref[0, h]

    def softmax_pv(h, s_ref, ss):
        sl = slice(h * HEAD_DIM, (h + 1) * HEAD_DIM)
        s = s_ref[...]
        m = jnp.max(s, axis=-1, keepdims=True)
        p = jnp.exp2(s - m)
        v = with_meta(v_refs, vm_ref, sl)
        v1 = jnp.concatenate([v, jnp.ones_like(v)], axis=1)
        o2 = jnp.dot(p.astype(v.dtype), v1, preferred_element_type=jnp.float32)
        o = o2[:, :HEAD_DIM] / o2[:, HEAD_DIM:HEAD_DIM + 1]
        oacc_ref[:, sl] = o
        return ss + jnp.sum(o * o, axis=-1, keepdims=True)

    bufs = (sa_ref, sb_ref)
    ss = jnp.zeros((nq, 1), jnp.float32)
    logits(0, bufs[0])
    for h in range(N_NA_HEADS):
        if h + 1 < N_NA_HEADS:
            logits(h + 1, bufs[(h + 1) % 2])
        ss = softmax_pv(h, bufs[h % 2], ss)
    _group_norm_store(oacc_ref, ss, g_ref, o_ref)


def _na_attention(proj, projm, bias, g, tables):
    n = proj.shape[0]
    nblk = n // TOK_BLOCK
    nq = NA_STEP_BLOCKS * TOK_BLOCK
    qblk = (nq, D_NA)
    kblk = (TOK_BLOCK, D_NA)

    def kv_spec(col, j):
        return pl.BlockSpec(kblk, lambda i, kb, pat: (jnp.minimum(kb[i] + j, nblk - 1), col))

    in_specs = [pl.BlockSpec(qblk, lambda i, kb, pat: (i, 0))]
    in_specs += [kv_spec(1, j) for j in range(NA_KEY_BLOCKS)]
    in_specs += [kv_spec(2, j) for j in range(NA_KEY_BLOCKS)]
    in_specs += [
        pl.BlockSpec(kblk, lambda i, kb, pat: (0, 0)),
        pl.BlockSpec(kblk, lambda i, kb, pat: (0, 1)),
        pl.BlockSpec((1, N_NA_HEADS, nq, NA_KEYS), lambda i, kb, pat: (pat[i], 0, 0, 0)),
        pl.BlockSpec((1, D_NA), lambda i, kb, pat: (0, 0)),
    ]
    return pl.pallas_call(
        _na_kernel,
        grid_spec=pltpu.PrefetchScalarGridSpec(
            num_scalar_prefetch=2,
            grid=(n // nq,),
            in_specs=in_specs,
            out_specs=pl.BlockSpec(qblk, lambda i, kb, pat: (i, 0)),
            scratch_shapes=[pltpu.VMEM(qblk, jnp.float32),
                            pltpu.VMEM((nq, NA_KEYS), jnp.float32),
                            pltpu.VMEM((nq, NA_KEYS), jnp.float32)],
        ),
        out_shape=jax.ShapeDtypeStruct((n, D_NA), jnp.bfloat16),
        compiler_params=_cparams(("arbitrary",)),
        name="na_attention",
    )(tables["kb"], tables["pat"], proj, *([proj] * (2 * NA_KEY_BLOCKS)), projm, projm, bias, g)


def _swa_bias_kernel(slope_ref, o_ref):
    slope = slope_ref[pl.program_id(0)]
    shape = (SW_BLOCK, SW_KEYS)
    r = lax.broadcasted_iota(jnp.int32, shape, 0)
    c = lax.broadcasted_iota(jnp.int32, shape, 1)
    n_real = 3 * SW_BLOCK
    rel = c - r
    band = (rel >= 0) & (rel <= 2 * SW_WINDOW)
    is_real = c < n_real
    j = c - n_real
    is_meta = (c >= n_real) & (j < N_META)
    for p in range(4):
        first, last = p & 1, p >> 1
        c_lo = SW_BLOCK if first else 0
        c_hi = 2 * SW_BLOCK if last else 3 * SW_BLOCK
        valid = (band & (c >= c_lo) & (c < c_hi)) | is_meta
        mdist = jnp.minimum(N_META + r - j, SW_WINDOW) if first else jnp.full(shape, SW_WINDOW, jnp.int32)
        dist = jnp.where(is_real, jnp.abs(rel - SW_BLOCK), mdist).astype(jnp.float32)
        o_ref[p, 0] = jnp.where(valid, -slope * dist, NEG_INF)


def _swa_bias(slopes):
    return pl.pallas_call(
        _swa_bias_kernel,
        grid=(N_SW_HEADS,),
        in_specs=[pl.BlockSpec(memory_space=pltpu.SMEM)],
        out_specs=pl.BlockSpec((4, 1, SW_BLOCK, SW_KEYS), lambda h: (0, h, 0, 0)),
        out_shape=jax.ShapeDtypeStruct((4, N_SW_HEADS, SW_BLOCK, SW_KEYS), jnp.float32),
        compiler_params=_cparams(("parallel",)),
        name="swa_bias",
    )(slopes)


def _swa_kernel(prev_ref, nxt_ref, pat_ref, sink_ref, q_ref, kp_ref, kc_ref, kn_ref,
                vp_ref, vc_ref, vn_ref, km_ref, vm_ref, bias_ref, g_ref, o_ref, oacc_ref, sa_ref, sb_ref):
    def group_heads(kh):
        return [kh * SW_GROUP + gi for gi in range(SW_GROUP)]

    def logits(kh, s_ref):
        ksl = slice(kh * HEAD_DIM, (kh + 1) * HEAD_DIM)
        k = jnp.concatenate([kp_ref[:, ksl], kc_ref[:, ksl], kn_ref[:, ksl], km_ref[:, ksl]], axis=0)
        q = jnp.concatenate([q_ref[:, h * HEAD_DIM:(h + 1) * HEAD_DIM] for h in group_heads(kh)], axis=0)
        s_all = lax.dot_general(q, k, _NT, preferred_element_type=jnp.float32)
        for gi, h in enumerate(group_heads(kh)):
            rows = slice(gi * SW_BLOCK, (gi + 1) * SW_BLOCK)
            s_ref[rows] = s_all[rows] + bias_ref[0, h]

    def softmax_pv(kh, s_ref, ss):
        ksl = slice(kh * HEAD_DIM, (kh + 1) * HEAD_DIM)
        v = jnp.concatenate([vp_ref[:, ksl], vc_ref[:, ksl], vn_ref[:, ksl], vm_ref[:, ksl]], axis=0)
        ps, ls = [], []
        for gi, h in enumerate(group_heads(kh)):
            s = s_ref[gi * SW_BLOCK:(gi + 1) * SW_BLOCK]
            sink = sink_ref[h] * LOG2E
            m = jnp.maximum(jnp.max(s, axis=-1, keepdims=True), sink)
            p = jnp.exp2(s - m)
            ls.append(jnp.sum(p, axis=-1, keepdims=True) + jnp.exp2(sink - m))
            ps.append(p.astype(v.dtype))
        o_all = jnp.dot(jnp.concatenate(ps, axis=0), v, preferred_element_type=jnp.float32)
        for gi, h in enumerate(group_heads(kh)):
            o = o_all[gi * SW_BLOCK:(gi + 1) * SW_BLOCK] / ls[gi]
            ss = ss + jnp.sum(o * o, axis=-1, keepdims=True)
            oacc_ref[:, h * HEAD_DIM:(h + 1) * HEAD_DIM] = o
        return ss

    bufs = (sa_ref, sb_ref)
    ss = jnp.zeros((SW_BLOCK, 1), jnp.float32)
    logits(0, bufs[0])
    for kh in range(N_SW_KV_HEADS):
        if kh + 1 < N_SW_KV_HEADS:
            logits(kh + 1, bufs[(kh + 1) % 2])
        ss = softmax_pv(kh, bufs[kh % 2], ss)
    _group_norm_store(oacc_ref, ss, g_ref, o_ref)


def _swa_attention(proj, projm, sink, bias, g, tables):
    n = proj.shape[0]
    nblk = n // SW_BLOCK
    qcol = 3 * D_NA // D_SW
    kcol = (3 * D_NA + D_SW) // D_SW_KV
    vcol = kcol + 1
    qblk = (SW_BLOCK, D_SW)
    kvblk = (SW_BLOCK, D_SW_KV)

    def kv(col, which):
        if which == "prev":
            return pl.BlockSpec(kvblk, lambda i, pv, nx, pt: (pv[i], col))
        if which == "cur":
            return pl.BlockSpec(kvblk, lambda i, pv, nx, pt: (i, col))
        if which == "next":
            return pl.BlockSpec(kvblk, lambda i, pv, nx, pt: (nx[i], col))
        return pl.BlockSpec(kvblk, lambda i, pv, nx, pt: (0, col))

    in_specs = [pl.BlockSpec(memory_space=pltpu.SMEM),
                pl.BlockSpec(qblk, lambda i, pv, nx, pt: (i, qcol))]
    in_specs += [kv(kcol, w) for w in ("prev", "cur", "next")]
    in_specs += [kv(vcol, w) for w in ("prev", "cur", "next")]
    mcol = 2 * D_NA // D_SW_KV
    in_specs += [kv(mcol, "meta"), kv(mcol + 1, "meta"),
                 pl.BlockSpec((1, N_SW_HEADS, SW_BLOCK, SW_KEYS), lambda i, pv, nx, pt: (pt[i], 0, 0, 0)),
                 pl.BlockSpec((1, D_SW), lambda i, pv, nx, pt: (0, 0))]
    return pl.pallas_call(
        _swa_kernel,
        grid_spec=pltpu.PrefetchScalarGridSpec(
            num_scalar_prefetch=3,
            grid=(nblk,),
            in_specs=in_specs,
            out_specs=pl.BlockSpec(qblk, lambda i, pv, nx, pt: (i, 0)),
            scratch_shapes=[pltpu.VMEM(qblk, jnp.float32),
                            pltpu.VMEM((SW_GROUP * SW_BLOCK, SW_KEYS), jnp.float32),
                            pltpu.VMEM((SW_GROUP * SW_BLOCK, SW_KEYS), jnp.float32)],
        ),
        out_shape=jax.ShapeDtypeStruct((n, D_SW), jnp.bfloat16),
        compiler_params=_cparams(("arbitrary",)),
        name="swa_attention",
    )(tables["prev"], tables["nxt"], tables["swpat"], sink, proj,
      proj, proj, proj, proj, proj, proj, projm, projm, bias, g)


def kernel(x_prompt, x_sample, meta_tokens, w_in, w_out, rpb, sink, g_na_out, g_sw_out,
           g_pre_mix, g_post_mix, g_pre_ffn, g_post_ffn, w_gate, w_up, w_down):
    xp = x_prompt.reshape(-1, D_MODEL)
    xs = x_sample.reshape(-1, D_MODEL)
    n_p = xp.shape[0]
    seq_blocks = [x_prompt.shape[1] // TOK_BLOCK] * x_prompt.shape[0]
    seq_blocks += [x_sample.shape[1] // TOK_BLOCK] * x_sample.shape[0]
    tables, patterns = _block_tables(seq_blocks)
    row = lambda v: v.reshape(1, -1)

    scale = HEAD_DIM ** -0.5 * LOG2E
    colscale = np.ones((1, D_IN), np.float32)
    colscale[:, :D_NA] = scale
    colscale[:, 3 * D_NA:3 * D_NA + D_SW] = scale
    colscale = jnp.asarray(colscale)

    h = _prenorm2(xp, xs, row(g_pre_mix[0]))
    proj = _mm_in(h, w_in[0], colscale, bm=MM_TILE)
    gap = jnp.zeros((TOK_BLOCK - 2 * N_META, D_MODEL), meta_tokens.dtype)
    meta_blk = jnp.concatenate([meta_tokens, gap, meta_tokens], axis=0)
    hm = _prenorm1(meta_blk, row(g_pre_mix[0]))
    bn = 2 * D_SW_KV
    n_kv, kv0, sw_kv = 2 * D_NA // bn, D_NA // bn, (3 * D_NA + D_SW) // bn
    projm = _mm_in(hm, w_in[0], colscale, bm=TOK_BLOCK, bn=bn, name="mm_in_meta", col_steps=n_kv + 1,
                   col_of=lambda j: jnp.where(j < n_kv, j + kv0, sw_kv))

    bias = _na_bias(rpb[0].reshape(-1), patterns)
    o_na = _na_attention(proj, projm, bias, row(g_na_out[0]), tables)
    slopes = np.asarray([2.0 ** (-8.0 * (i + 1) / N_SW_HEADS) * LOG2E for i in range(N_SW_HEADS)], np.float32)
    o_sw = _swa_attention(proj, projm, sink[0], _swa_bias(jnp.asarray(slopes)), row(g_sw_out[0]), tables)

    mix = _mm_out(o_na, o_sw, w_out[0])
    h2 = _post1(xp, xs, mix, row(g_post_mix[0]), row(g_pre_ffn[0]))

    act_main = _mm_gateup(h2, w_gate[0], w_up[0], 0, FF_MAIN, GATEUP_BN, "mm_gateup")
    act_tail = _mm_gateup(h2, w_gate[0], w_up[0], FF_MAIN, FF_TAIL, FF_TAIL, "mm_gateup_tail")
    f = _mm_down(act_main, act_tail, w_down[0])
    y_p = _post2(xp, mix, f, row(g_post_mix[0]), row(g_post_ffn[0]), 0)
    y_s = _post2(xs, mix, f, row(g_post_mix[0]), row(g_post_ffn[0]), n_p)
    return (y_p.reshape(x_prompt.shape), y_s.reshape(x_sample.shape))
```

```python
import functools
import math

import numpy as np
import jax
import jax.numpy as jnp
from jax import lax
from jax.experimental import pallas as pl
from jax.experimental.pallas import tpu as pltpu

D_MODEL = 4096
DEPTH = 1
HEAD_DIM = 128
N_HEADS = D_MODEL // HEAD_DIM
N_NA_HEADS = N_HEADS // 2
N_SW_HEADS = N_HEADS - N_NA_HEADS
N_SW_KV_HEADS = max(1, N_SW_HEADS // 4)
SW_GROUP = N_SW_HEADS // N_SW_KV_HEADS
D_NA = N_NA_HEADS * HEAD_DIM
D_SW = N_SW_HEADS * HEAD_DIM
D_SW_KV = N_SW_KV_HEADS * HEAD_DIM
D_IN = 3 * D_NA + D_SW + 2 * D_SW_KV
D_FF = -(-8 * D_MODEL // (3 * 256)) * 256
GRID_W = 64
NA_MAX_ROWS = 8
NA_COLS = 16
NA_QROWS = 2
SW_WINDOW = 128
SW_BLOCK = 128
N_META = 16
EPS = 1e-6
NEG_INF = -1e30

LANES = 128
BF16_SUBLANES = 16
TOK_BLOCK = NA_QROWS * GRID_W
NA_STEP_BLOCKS = 2
NA_STEP_ROWS = NA_STEP_BLOCKS * NA_QROWS
NA_KEY_ROWS = NA_STEP_ROWS + NA_MAX_ROWS
NA_KEY_BLOCKS = NA_KEY_ROWS * GRID_W // TOK_BLOCK
NA_KEYS = NA_KEY_BLOCKS * TOK_BLOCK
SW_HALF_KEYS = 3 * SW_BLOCK
MM_TILE = 1024
GATEUP_BN = 512
FF_MAIN = D_FF // GATEUP_BN * GATEUP_BN
FF_TAIL = D_FF - FF_MAIN
PRENORM_ROWS = 512
POST_ROWS = 256
VMEM_LIMIT = 56 * 1024 * 1024
VMEM_SLACK = 2 * 1024 * 1024
LOG2E = math.log2(math.e)

assert SW_BLOCK == TOK_BLOCK and DEPTH == 1 and FF_TAIL == 256


def _cparams(sem, vmem=VMEM_LIMIT):
    return pltpu.CompilerParams(dimension_semantics=sem, vmem_limit_bytes=vmem)


def _ew_vmem(tb, d, elem_bytes):
    return 2 * tb * d * sum(elem_bytes) + 2 * tb * d * 4 + VMEM_SLACK


def _block_tables(seq_blocks):
    kb, pat, prev, nxt, swpat = [], [], [], [], []
    patterns = []
    start_blk = 0
    for nb in seq_blocks:
        rows = nb * NA_QROWS
        assert nb % NA_STEP_BLOCKS == 0 and rows >= NA_KEY_ROWS
        for st in range(nb // NA_STEP_BLOCKS):
            r0 = st * NA_STEP_ROWS
            ks = max(r0 - NA_MAX_ROWS // 2, 0)
            assert ks % NA_QROWS == 0
            w = tuple(int(np.clip(r0 + a - NA_MAX_ROWS // 2, 0, rows - NA_MAX_ROWS)) - ks
                      for a in range(NA_STEP_ROWS))
            assert all(0 <= wa and wa + NA_MAX_ROWS <= NA_KEY_ROWS - 1 for wa in w)
            key = (r0 - ks, w)
            if key not in patterns:
                patterns.append(key)
            kb.append(start_blk + ks // NA_QROWS)
            pat.append(patterns.index(key))
        for bl in range(nb):
            prev.append(start_blk + max(bl - 1, 0))
            nxt.append(start_blk + min(bl + 1, nb - 1))
            swpat.append(int(bl == 0) + 2 * int(bl == nb - 1))
        start_blk += nb
    i32 = lambda v: jnp.asarray(np.asarray(v, np.int32))
    return dict(kb=i32(kb), pat=i32(pat), prev=i32(prev), nxt=i32(nxt), swpat=i32(swpat)), patterns


def _rms(x, g):
    return x * lax.rsqrt(jnp.mean(x * x, axis=-1, keepdims=True) + EPS) * g


def _prenorm2_kernel(xp_ref, xs_ref, g_ref, o_ref, *, n_first):
    i = pl.program_id(0)

    @pl.when(i < n_first)
    def _():
        o_ref[...] = _rms(xp_ref[...], g_ref[...]).astype(o_ref.dtype)

    @pl.when(i >= n_first)
    def _():
        o_ref[...] = _rms(xs_ref[...], g_ref[...]).astype(o_ref.dtype)


def _prenorm2(xp, xs, g, tb=PRENORM_ROWS):
    n_first, n_second = xp.shape[0] // tb, xs.shape[0] // tb
    d = xp.shape[1]
    return pl.pallas_call(
        functools.partial(_prenorm2_kernel, n_first=n_first),
        grid=(n_first + n_second,),
        in_specs=[
            pl.BlockSpec((tb, d), lambda i: (jnp.minimum(i, n_first - 1), 0)),
            pl.BlockSpec((tb, d), lambda i: (jnp.maximum(i - n_first, 0), 0)),
            pl.BlockSpec((1, d), lambda i: (0, 0)),
        ],
        out_specs=pl.BlockSpec((tb, d), lambda i: (i, 0)),
        out_shape=jax.ShapeDtypeStruct((xp.shape[0] + xs.shape[0], d), jnp.bfloat16),
        compiler_params=_cparams(("arbitrary",), _ew_vmem(tb, d, (4, 4, 2))),
        name="prenorm",
    )(xp, xs, g)


def _prenorm1_kernel(x_ref, g_ref, o_ref):
    o_ref[...] = _rms(x_ref[...], g_ref[...]).astype(o_ref.dtype)


def _prenorm1(x, g):
    n, d = x.shape
    return pl.pallas_call(
        _prenorm1_kernel,
        grid=(1,),
        in_specs=[pl.BlockSpec((n, d), lambda i: (0, 0)), pl.BlockSpec((1, d), lambda i: (0, 0))],
        out_specs=pl.BlockSpec((n, d), lambda i: (0, 0)),
        out_shape=jax.ShapeDtypeStruct((n, d), jnp.bfloat16),
        compiler_params=_cparams(("arbitrary",)),
        name="prenorm_meta",
    )(x, g)


def _mm_in_kernel(a_ref, b_ref, s_ref, o_ref):
    acc = jnp.dot(a_ref[...], b_ref[...].astype(jnp.bfloat16), preferred_element_type=jnp.float32)
    o_ref[...] = (acc * s_ref[...]).astype(o_ref.dtype)


def _mm_in(a, b, colscale, bm, bn=MM_TILE, name="mm_in", col_steps=None, col_of=lambda j: j):
    m, k = a.shape
    col_steps = b.shape[1] // bn if col_steps is None else col_steps
    return pl.pallas_call(
        _mm_in_kernel,
        grid=(m // bm, col_steps),
        in_specs=[
            pl.BlockSpec((bm, k), lambda i, j: (i, 0)),
            pl.BlockSpec((k, bn), lambda i, j: (0, col_of(j))),
            pl.BlockSpec((1, bn), lambda i, j: (0, col_of(j))),
        ],
        out_specs=pl.BlockSpec((bm, bn), lambda i, j: (i, j)),
        out_shape=jax.ShapeDtypeStruct((m, col_steps * bn), jnp.bfloat16),
        compiler_params=_cparams(("parallel", "parallel"),
                                 2 * bm * k * 2 + 2 * k * bn * 4 + k * bn * 2 + 2 * bm * bn * 2 + VMEM_SLACK),
        name=name,
    )(a, b, colscale)


def _mm_out_kernel(a1_ref, a2_ref, b1_ref, b2_ref, o_ref):
    acc = jnp.dot(a1_ref[...], b1_ref[...].astype(jnp.bfloat16), preferred_element_type=jnp.float32)
    acc = acc + jnp.dot(a2_ref[...], b2_ref[...].astype(jnp.bfloat16), preferred_element_type=jnp.float32)
    o_ref[...] = acc.astype(o_ref.dtype)


def _mm_out(a1, a2, b, bm=MM_TILE, bn=MM_TILE):
    m, k1 = a1.shape
    k2 = a2.shape[1]
    n = b.shape[1]
    assert k1 == k2 and b.shape[0] == k1 + k2
    vmem = 4 * bm * k1 * 2 + 4 * k1 * bn * 4 + 2 * k1 * bn * 2 + 2 * bm * bn * 2 + VMEM_SLACK
    return pl.pallas_call(
        _mm_out_kernel,
        grid=(m // bm, n // bn),
        in_specs=[
            pl.BlockSpec((bm, k1), lambda i, j: (i, 0)),
            pl.BlockSpec((bm, k2), lambda i, j: (i, 0)),
            pl.BlockSpec((k1, bn), lambda i, j: (0, j)),
            pl.BlockSpec((k2, bn), lambda i, j: (1, j)),
        ],
        out_specs=pl.BlockSpec((bm, bn), lambda i, j: (i, j)),
        out_shape=jax.ShapeDtypeStruct((m, n), jnp.bfloat16),
        compiler_params=_cparams(("parallel", "parallel"), vmem),
        name="mm_out",
    )(a1, a2, b, b)


def _mm_gateup_kernel(a_ref, bg_ref, bu_ref, o_ref):
    a = a_ref[...]
    g = jnp.dot(a, bg_ref[...].astype(jnp.bfloat16), preferred_element_type=jnp.float32)
    u = jnp.dot(a, bu_ref[...].astype(jnp.bfloat16), preferred_element_type=jnp.float32)
    o_ref[...] = (g / (1.0 + jnp.exp(-g)) * u).astype(o_ref.dtype)


def _mm_gateup(a, bg, bu, col0, ncols, bn, name, bm=MM_TILE):
    m, k = a.shape
    assert col0 % bn == 0 and ncols % bn == 0
    j0 = col0 // bn
    wspec = pl.BlockSpec((k, bn), lambda i, j: (0, j + j0))
    return pl.pallas_call(
        _mm_gateup_kernel,
        grid=(m // bm, ncols // bn),
        in_specs=[pl.BlockSpec((bm, k), lambda i, j: (i, 0)), wspec, wspec],
        out_specs=pl.BlockSpec((bm, bn), lambda i, j: (i, j)),
        out_shape=jax.ShapeDtypeStruct((m, ncols), jnp.bfloat16),
        compiler_params=_cparams(("parallel", "parallel"),
                                 2 * bm * k * 2 + 4 * k * bn * 4 + 2 * k * bn * 2 + 2 * bm * bn * 2 + VMEM_SLACK),
        name=name,
    )(a, bg, bu)


def _mm_down_kernel(a1_ref, a2_ref, b1_ref, b2_ref, o_ref, acc_ref):
    kk = pl.program_id(2)

    @pl.when(kk == 0)
    def _():
        acc_ref[...] = jnp.dot(a2_ref[...], b2_ref[...].astype(jnp.bfloat16), preferred_element_type=jnp.float32)

    acc_ref[...] += jnp.dot(a1_ref[...], b1_ref[...].astype(jnp.bfloat16), preferred_element_type=jnp.float32)

    @pl.when(kk == pl.num_programs(2) - 1)
    def _():
        o_ref[...] = acc_ref[...].astype(o_ref.dtype)


def _mm_down(a1, a2, b, bm=MM_TILE, bn=MM_TILE, ksteps=3):
    m, k1 = a1.shape
    k2 = a2.shape[1]
    n = b.shape[1]
    bk = k1 // ksteps
    assert b.shape[0] == k1 + k2 and k1 % k2 == 0 and k1 % ksteps == 0 and bk % LANES == 0
    vmem = (2 * bm * (bk + k2) * 2 + 2 * (bk + k2) * bn * 4 + (bk + k2) * bn * 2 + 2 * bm * bn * 2
            + bm * bn * 4 + VMEM_SLACK)
    return pl.pallas_call(
        _mm_down_kernel,
        grid=(m // bm, n // bn, ksteps),
        in_specs=[
            pl.BlockSpec((bm, bk), lambda i, j, kk: (i, kk)),
            pl.BlockSpec((bm, k2), lambda i, j, kk: (i, 0)),
            pl.BlockSpec((bk, bn), lambda i, j, kk: (kk, j)),
            pl.BlockSpec((k2, bn), lambda i, j, kk: (k1 // k2, j)),
        ],
        out_specs=pl.BlockSpec((bm, bn), lambda i, j, kk: (i, j)),
        out_shape=jax.ShapeDtypeStruct((m, n), jnp.bfloat16),
        scratch_shapes=[pltpu.VMEM((bm, bn), jnp.float32)],
        compiler_params=_cparams(("parallel", "parallel", "arbitrary"), vmem),
        name="mm_down",
    )(a1, a2, b, b)


def _post1_kernel(xp_ref, xs_ref, mix_ref, gpost_ref, gpre_ref, h_ref, *, n_first):
    i = pl.program_id(0)

    def body(x_ref):
        x1 = x_ref[...] + _rms(mix_ref[...].astype(jnp.float32), gpost_ref[...])
        h_ref[...] = _rms(x1, gpre_ref[...]).astype(h_ref.dtype)

    @pl.when(i < n_first)
    def _():
        body(xp_ref)

    @pl.when(i >= n_first)
    def _():
        body(xs_ref)


def _post1(xp, xs, mix, gpost, gpre, tb=POST_ROWS):
    n_first, n_second = xp.shape[0] // tb, xs.shape[0] // tb
    d = xp.shape[1]
    n = mix.shape[0]
    row = pl.BlockSpec((tb, d), lambda i: (i, 0))
    vec = pl.BlockSpec((1, d), lambda i: (0, 0))
    return pl.pallas_call(
        functools.partial(_post1_kernel, n_first=n_first),
        grid=(n_first + n_second,),
        in_specs=[
            pl.BlockSpec((tb, d), lambda i: (jnp.minimum(i, n_first - 1), 0)),
            pl.BlockSpec((tb, d), lambda i: (jnp.maximum(i - n_first, 0), 0)),
            row, vec, vec,
        ],
        out_specs=row,
        out_shape=jax.ShapeDtypeStruct((n, d), jnp.bfloat16),
        compiler_params=_cparams(("arbitrary",), _ew_vmem(tb, d, (4, 4, 2, 2))),
        name="post_mix",
    )(xp, xs, mix, gpost, gpre)


def _post2_kernel(x_ref, mix_ref, f_ref, gmix_ref, gffn_ref, o_ref):
    x1 = x_ref[...] + _rms(mix_ref[...].astype(jnp.float32), gmix_ref[...])
    o_ref[...] = x1 + _rms(f_ref[...].astype(jnp.float32), gffn_ref[...])


def _post2(x, mix, f, gmix, gffn, row0, tb=POST_ROWS):
    nrows, d = x.shape
    b0 = row0 // tb
    own = pl.BlockSpec((tb, d), lambda i: (i, 0))
    src = pl.BlockSpec((tb, d), lambda i: (i + b0, 0))
    vec = pl.BlockSpec((1, d), lambda i: (0, 0))
    return pl.pallas_call(
        _post2_kernel,
        grid=(nrows // tb,),
        in_specs=[own, src, src, vec, vec],
        out_specs=own,
        out_shape=jax.ShapeDtypeStruct((nrows, d), jnp.float32),
        compiler_params=_cparams(("parallel",), _ew_vmem(tb, d, (4, 2, 2, 4))),
        name="post_ffn",
    )(x, mix, f, gmix, gffn)


def _na_bias_kernel(rpb_ref, o_ref, *, patterns):
    h = pl.program_id(0)
    n_ic = 2 * NA_COLS - 1
    n_ir = 2 * NA_MAX_ROWS - 1
    shape = (GRID_W, LANES)
    qc = lax.broadcasted_iota(jnp.int32, shape, 0)
    lane = lax.broadcasted_iota(jnp.int32, shape, 1)
    kc = lane & (GRID_W - 1)
    delta = kc - qc + (NA_COLS - 1)
    cq = jnp.clip(qc - NA_COLS // 2, 0, GRID_W - NA_COLS)
    col_ok = (kc >= cq) & (kc < cq + NA_COLS)
    left = lane < GRID_W
    neg = jnp.full(shape, NEG_INF, jnp.float32)

    w = []
    for e in range(n_ir):
        acc = jnp.zeros(shape, jnp.float32)
        for d in range(n_ic):
            acc = jnp.where(delta == d, rpb_ref[h * (n_ir * n_ic) + e * n_ic + d], acc)
        w.append(acc * LOG2E)
    zero = jnp.zeros(shape, jnp.float32)

    def tile(dr):
        e = dr + NA_MAX_ROWS - 1
        return w[e] if 0 <= e < n_ir else zero

    for p, (rel, win) in enumerate(patterns):
        for a in range(NA_STEP_ROWS):
            for c in range(NA_KEY_BLOCKS):
                b0 = 2 * c
                dr = b0 - a - rel
                ok0 = win[a] <= b0 < win[a] + NA_MAX_ROWS
                ok1 = win[a] <= b0 + 1 < win[a] + NA_MAX_ROWS
                pair = jnp.where(left, tile(dr), tile(dr + 1))
                if ok0 and ok1:
                    ok = col_ok
                elif ok0:
                    ok = col_ok & left
                elif ok1:
                    ok = col_ok & jnp.logical_not(left)
                else:
                    ok = None
                val = neg if ok is None else jnp.where(ok, pair, neg)
                if c == NA_KEY_BLOCKS - 1:
                    assert not ok1
                    val = jnp.where(lane >= LANES - N_META, 0.0, val)
                o_ref[p, 0, a * GRID_W:(a + 1) * GRID_W, c * LANES:(c + 1) * LANES] = val


def _na_bias(rpb_flat, patterns):
    npat = len(patterns)
    q = NA_STEP_BLOCKS * TOK_BLOCK
    return pl.pallas_call(
        functools.partial(_na_bias_kernel, patterns=tuple(patterns)),
        grid=(N_NA_HEADS,),
        in_specs=[pl.BlockSpec(memory_space=pltpu.SMEM)],
        out_specs=pl.BlockSpec((npat, 1, q, NA_KEYS), lambda h: (0, h, 0, 0)),
        out_shape=jax.ShapeDtypeStruct((npat, N_NA_HEADS, q, NA_KEYS), jnp.float32),
        compiler_params=_cparams(("parallel",)),
        name="na_bias",
    )(rpb_flat)


_NT = (((1,), (1,)), ((), ()))


def _group_norm_store(oacc_ref, ss, g_ref, o_ref):
    inv = lax.rsqrt(ss * (1.0 / oacc_ref.shape[1]) + EPS)
    o_ref[...] = (oacc_ref[...] * inv * g_ref[...]).astype(o_ref.dtype)


def _na_kernel(kb_ref, pat_ref, q_ref, *rest):
    k_refs = rest[:NA_KEY_BLOCKS]
    v_refs = rest[NA_KEY_BLOCKS:2 * NA_KEY_BLOCKS]
    km_ref, vm_ref, bias_ref, g_ref, o_ref, oacc_ref, sa_ref, sb_ref = rest[2 * NA_KEY_BLOCKS:]
    nq = q_ref.shape[0]
    real = TOK_BLOCK - BF16_SUBLANES

    def with_meta(refs, m_ref, sl):
        return jnp.concatenate([r[:, sl] for r in refs[:-1]] + [refs[-1][:real, sl], m_ref[real:, sl]], axis=0)

    def logits(h, s_ref):
        sl = slice(h * HEAD_DIM, (h + 1) * HEAD_DIM)
        k = with_meta(k_refs, km_ref, sl)
        s_ref[...] = lax.dot_general(q_ref[:, sl], k, _NT, preferred_element_type=jnp.float32) + bias_ref[0, h]

    def softmax_pv(h, s_ref, ss):
        sl = slice(h * HEAD_DIM, (h + 1) * HEAD_DIM)
        s = s_ref[...]
        m = jnp.max(s, axis=-1, keepdims=True)
        p = jnp.exp2(s - m)
        v = with_meta(v_refs, vm_ref, sl)
        v1 = jnp.concatenate([v, jnp.ones_like(v)], axis=1)
        o2 = jnp.dot(p.astype(v.dtype), v1, preferred_element_type=jnp.float32)
        o = o2[:, :HEAD_DIM] / o2[:, HEAD_DIM:HEAD_DIM + 1]
        oacc_ref[:, sl] = o
        return ss + jnp.sum(o * o, axis=-1, keepdims=True)

    bufs = (sa_ref, sb_ref)
    ss = jnp.zeros((nq, 1), jnp.float32)
    logits(0, bufs[0])
    for h in range(N_NA_HEADS):
        if h + 1 < N_NA_HEADS:
            logits(h + 1, bufs[(h + 1) % 2])
        ss = softmax_pv(h, bufs[h % 2], ss)
    _group_norm_store(oacc_ref, ss, g_ref, o_ref)


def _na_attention(proj, projm, bias, g, tables):
    n = proj.shape[0]
    nblk = n // TOK_BLOCK
    nq = NA_STEP_BLOCKS * TOK_BLOCK
    qblk = (nq, D_NA)
    kblk = (TOK_BLOCK, D_NA)

    def kv_spec(col, j):
        return pl.BlockSpec(kblk, lambda i, kb, pat: (jnp.minimum(kb[i] + j, nblk - 1), col))

    in_specs = [pl.BlockSpec(qblk, lambda i, kb, pat: (i, 0))]
    in_specs += [kv_spec(1, j) for j in range(NA_KEY_BLOCKS)]
    in_specs += [kv_spec(2, j) for j in range(NA_KEY_BLOCKS)]
    in_specs += [
        pl.BlockSpec(kblk, lambda i, kb, pat: (0, 0)),
        pl.BlockSpec(kblk, lambda i, kb, pat: (0, 1)),
        pl.BlockSpec((1, N_NA_HEADS, nq, NA_KEYS), lambda i, kb, pat: (pat[i], 0, 0, 0)),
        pl.BlockSpec((1, D_NA), lambda i, kb, pat: (0, 0)),
    ]
    return pl.pallas_call(
        _na_kernel,
        grid_spec=pltpu.PrefetchScalarGridSpec(
            num_scalar_prefetch=2,
            grid=(n // nq,),
            in_specs=in_specs,
            out_specs=pl.BlockSpec(qblk, lambda i, kb, pat: (i, 0)),
            scratch_shapes=[pltpu.VMEM(qblk, jnp.float32),
                            pltpu.VMEM((nq, NA_KEYS), jnp.float32),
                            pltpu.VMEM((nq, NA_KEYS), jnp.float32)],
        ),
        out_shape=jax.ShapeDtypeStruct((n, D_NA), jnp.bfloat16),
        compiler_params=_cparams(("arbitrary",)),
        name="na_attention",
    )(tables["kb"], tables["pat"], proj, *([proj] * (2 * NA_KEY_BLOCKS)), projm, projm, bias, g)


def _swa_bias_kernel(slope_ref, o_ref):
    slope = slope_ref[pl.program_id(0)]
    shape = (SW_BLOCK, SW_HALF_KEYS)
    half = SW_BLOCK // 2
    r = lax.broadcasted_iota(jnp.int32, shape, 0)
    c = lax.broadcasted_iota(jnp.int32, shape, 1)
    lo = r < half
    hi = jnp.logical_not(lo)
    meta0_lo = 2 * SW_BLOCK + half
    meta0_hi = half
    is_real = (lo & (c < meta0_lo)) | (hi & ((c < half) | (c >= SW_BLOCK)))
    is_meta = (lo & (c >= meta0_lo) & (c < meta0_lo + N_META)) | (hi & (c >= meta0_hi) & (c < meta0_hi + N_META))
    kc = jnp.where(hi & (c < half), c + half, c)
    j = jnp.where(lo, c - meta0_lo, c - meta0_hi)
    rel = kc - r
    band = (rel >= 0) & (rel <= 2 * SW_WINDOW)
    for p in range(4):
        first, last = p & 1, p >> 1
        c_lo = SW_BLOCK if first else 0
        c_hi = 2 * SW_BLOCK if last else 3 * SW_BLOCK
        valid = (is_real & band & (kc >= c_lo) & (kc < c_hi)) | is_meta
        mdist = jnp.minimum(N_META + r - j, SW_WINDOW) if first else jnp.full(shape, SW_WINDOW, jnp.int32)
        dist = jnp.where(is_real, jnp.abs(rel - SW_BLOCK), mdist).astype(jnp.float32)
        o_ref[p, 0] = jnp.where(valid, -slope * dist, NEG_INF)


def _swa_bias(slopes):
    return pl.pallas_call(
        _swa_bias_kernel,
        grid=(N_SW_HEADS,),
        in_specs=[pl.BlockSpec(memory_space=pltpu.SMEM)],
        out_specs=pl.BlockSpec((4, 1, SW_BLOCK, SW_HALF_KEYS), lambda h: (0, h, 0, 0)),
        out_shape=jax.ShapeDtypeStruct((4, N_SW_HEADS, SW_BLOCK, SW_HALF_KEYS), jnp.float32),
        compiler_params=_cparams(("parallel",)),
        name="swa_bias",
    )(slopes)


def _swa_kernel(prev_ref, nxt_ref, pat_ref, sink_ref, q_ref, kp_ref, kc_ref, kn_ref,
                vp_ref, vc_ref, vn_ref, km_ref, vm_ref, bias_ref, g_ref, o_ref, oacc_ref, sa_ref, sb_ref):
    half = SW_BLOCK // 2
    nrow = SW_GROUP * half

    def group_heads(kh):
        return [kh * SW_GROUP + gi for gi in range(SW_GROUP)]

    def keys(p_ref, c_ref, n_ref, m_ref, ksl):
        lo = jnp.concatenate([p_ref[:, ksl], c_ref[:, ksl], n_ref[:half, ksl], m_ref[:half, ksl]], axis=0)
        hi = jnp.concatenate([p_ref[half:, ksl], m_ref[:half, ksl], c_ref[:, ksl], n_ref[:, ksl]], axis=0)
        return lo, hi

    def logits(kh, s_ref):
        ksl = slice(kh * HEAD_DIM, (kh + 1) * HEAD_DIM)
        k_lo, k_hi = keys(kp_ref, kc_ref, kn_ref, km_ref, ksl)
        for part, (k, r0) in enumerate(((k_lo, 0), (k_hi, half))):
            q = jnp.concatenate([q_ref[r0:r0 + half, h * HEAD_DIM:(h + 1) * HEAD_DIM] for h in group_heads(kh)], axis=0)
            s_all = lax.dot_general(q, k, _NT, preferred_element_type=jnp.float32)
            for gi, h in enumerate(group_heads(kh)):
                rows = slice(gi * half, (gi + 1) * half)
                s_ref[part * nrow + gi * half:part * nrow + (gi + 1) * half] = s_all[rows] + bias_ref[0, h, r0:r0 + half]

    def softmax_pv(kh, s_ref, ss):
        ksl = slice(kh * HEAD_DIM, (kh + 1) * HEAD_DIM)
        v_lo, v_hi = keys(vp_ref, vc_ref, vn_ref, vm_ref, ksl)
        new_ss = []
        for part, (v, r0) in enumerate(((v_lo, 0), (v_hi, half))):
            ps, ls = [], []
            for gi, h in enumerate(group_heads(kh)):
                s = s_ref[part * nrow + gi * half:part * nrow + (gi + 1) * half]
                sink = sink_ref[h] * LOG2E
                m = jnp.maximum(jnp.max(s, axis=-1, keepdims=True), sink)
                p = jnp.exp2(s - m)
                ls.append(jnp.sum(p, axis=-1, keepdims=True) + jnp.exp2(sink - m))
                ps.append(p.astype(v.dtype))
            o_all = jnp.dot(jnp.concatenate(ps, axis=0), v, preferred_element_type=jnp.float32)
            acc = ss[part]
            for gi, h in enumerate(group_heads(kh)):
                o = o_all[gi * half:(gi + 1) * half] / ls[gi]
                acc = acc + jnp.sum(o * o, axis=-1, keepdims=True)
                oacc_ref[r0:r0 + half, h * HEAD_DIM:(h + 1) * HEAD_DIM] = o
            new_ss.append(acc)
        return new_ss

    bufs = (sa_ref, sb_ref)
    ss = [jnp.zeros((half, 1), jnp.float32), jnp.zeros((half, 1), jnp.float32)]
    logits(0, bufs[0])
    for kh in range(N_SW_KV_HEADS):
        if kh + 1 < N_SW_KV_HEADS:
            logits(kh + 1, bufs[(kh + 1) % 2])
        ss = softmax_pv(kh, bufs[kh % 2], ss)
    _group_norm_store(oacc_ref, jnp.concatenate(ss, axis=0), g_ref, o_ref)


def _swa_attention(proj, projm, sink, bias, g, tables):
    n = proj.shape[0]
    nblk = n // SW_BLOCK
    qcol = 3 * D_NA // D_SW
    kcol = (3 * D_NA + D_SW) // D_SW_KV
    vcol = kcol + 1
    qblk = (SW_BLOCK, D_SW)
    kvblk = (SW_BLOCK, D_SW_KV)

    def kv(col, which):
        if which == "prev":
            return pl.BlockSpec(kvblk, lambda i, pv, nx, pt: (pv[i], col))
        if which == "cur":
            return pl.BlockSpec(kvblk, lambda i, pv, nx, pt: (i, col))
        if which == "next":
            return pl.BlockSpec(kvblk, lambda i, pv, nx, pt: (nx[i], col))
        return pl.BlockSpec(kvblk, lambda i, pv, nx, pt: (0, col))

    in_specs = [pl.BlockSpec(memory_space=pltpu.SMEM),
                pl.BlockSpec(qblk, lambda i, pv, nx, pt: (i, qcol))]
    in_specs += [kv(kcol, w) for w in ("prev", "cur", "next")]
    in_specs += [kv(vcol, w) for w in ("prev", "cur", "next")]
    mcol = 2 * D_NA // D_SW_KV
    in_specs += [kv(mcol, "meta"), kv(mcol + 1, "meta"),
                 pl.BlockSpec((1, N_SW_HEADS, SW_BLOCK, SW_HALF_KEYS), lambda i, pv, nx, pt: (pt[i], 0, 0, 0)),
                 pl.BlockSpec((1, D_SW), lambda i, pv, nx, pt: (0, 0))]
    return pl.pallas_call(
        _swa_kernel,
        grid_spec=pltpu.PrefetchScalarGridSpec(
            num_scalar_prefetch=3,
            grid=(nblk,),
            in_specs=in_specs,
            out_specs=pl.BlockSpec(qblk, lambda i, pv, nx, pt: (i, 0)),
            scratch_shapes=[pltpu.VMEM(qblk, jnp.float32),
                            pltpu.VMEM((SW_GROUP * SW_BLOCK, SW_HALF_KEYS), jnp.float32),
                            pltpu.VMEM((SW_GROUP * SW_BLOCK, SW_HALF_KEYS), jnp.float32)],
        ),
        out_shape=jax.ShapeDtypeStruct((n, D_SW), jnp.bfloat16),
        compiler_params=_cparams(("arbitrary",)),
        name="swa_attention",
    )(tables["prev"], tables["nxt"], tables["swpat"], sink, proj,
      proj, proj, proj, proj, proj, proj, projm, projm, bias, g)


def kernel(x_prompt, x_sample, meta_tokens, w_in, w_out, rpb, sink, g_na_out, g_sw_out,
           g_pre_mix, g_post_mix, g_pre_ffn, g_post_ffn, w_gate, w_up, w_down):
    xp = x_prompt.reshape(-1, D_MODEL)
    xs = x_sample.reshape(-1, D_MODEL)
    n_p = xp.shape[0]
    seq_blocks = [x_prompt.shape[1] // TOK_BLOCK] * x_prompt.shape[0]
    seq_blocks += [x_sample.shape[1] // TOK_BLOCK] * x_sample.shape[0]
    tables, patterns = _block_tables(seq_blocks)
    row = lambda v: v.reshape(1, -1)

    scale = HEAD_DIM ** -0.5 * LOG2E
    colscale = np.ones((1, D_IN), np.float32)
    colscale[:, :D_NA] = scale
    colscale[:, 3 * D_NA:3 * D_NA + D_SW] = scale
    colscale = jnp.asarray(colscale)

    h = _prenorm2(xp, xs, row(g_pre_mix[0]))
    proj = _mm_in(h, w_in[0], colscale, bm=MM_TILE)
    gap = jnp.zeros((TOK_BLOCK - 2 * N_META, D_MODEL), meta_tokens.dtype)
    meta_blk = jnp.concatenate([meta_tokens, gap, meta_tokens], axis=0)
    hm = _prenorm1(meta_blk, row(g_pre_mix[0]))
    bn = 2 * D_SW_KV
    n_kv, kv0, sw_kv = 2 * D_NA // bn, D_NA // bn, (3 * D_NA + D_SW) // bn
    projm = _mm_in(hm, w_in[0], colscale, bm=TOK_BLOCK, bn=bn, name="mm_in_meta", col_steps=n_kv + 1,
                   col_of=lambda j: jnp.where(j < n_kv, j + kv0, sw_kv))

    bias = _na_bias(rpb[0].reshape(-1), patterns)
    o_na = _na_attention(proj, projm, bias, row(g_na_out[0]), tables)
    slopes = np.asarray([2.0 ** (-8.0 * (i + 1) / N_SW_HEADS) * LOG2E for i in range(N_SW_HEADS)], np.float32)
    o_sw = _swa_attention(proj, projm, sink[0], _swa_bias(jnp.asarray(slopes)), row(g_sw_out[0]), tables)

    mix = _mm_out(o_na, o_sw, w_out[0])
    h2 = _post1(xp, xs, mix, row(g_post_mix[0]), row(g_pre_ffn[0]))

    act_main = _mm_gateup(h2, w_gate[0], w_up[0], 0, FF_MAIN, GATEUP_BN, "mm_gateup")
    act_tail = _mm_gateup(h2, w_gate[0], w_up[0], FF_MAIN, FF_TAIL, FF_TAIL, "mm_gateup_tail")
    f = _mm_down(act_main, act_tail, w_down[0])
    y_p = _post2(xp, mix, f, row(g_post_mix[0]), row(g_post_ffn[0]), 0)
    y_s = _post2(xs, mix, f, row(g_post_mix[0]), row(g_post_ffn[0]), n_p)
    return (y_p.reshape(x_prompt.shape), y_s.reshape(x_sample.shape))
```

```python
import functools
import math

import numpy as np
import jax
import jax.numpy as jnp
from jax import lax
from jax.experimental import pallas as pl
from jax.experimental.pallas import tpu as pltpu

D_MODEL = 4096
DEPTH = 1
HEAD_DIM = 128
N_HEADS = D_MODEL // HEAD_DIM
N_NA_HEADS = N_HEADS // 2
N_SW_HEADS = N_HEADS - N_NA_HEADS
N_SW_KV_HEADS = max(1, N_SW_HEADS // 4)
SW_GROUP = N_SW_HEADS // N_SW_KV_HEADS
D_NA = N_NA_HEADS * HEAD_DIM
D_SW = N_SW_HEADS * HEAD_DIM
D_SW_KV = N_SW_KV_HEADS * HEAD_DIM
D_IN = 3 * D_NA + D_SW + 2 * D_SW_KV
D_FF = -(-8 * D_MODEL // (3 * 256)) * 256
GRID_W = 64
NA_MAX_ROWS = 8
NA_COLS = 16
NA_QROWS = 2
SW_WINDOW = 128
SW_BLOCK = 128
N_META = 16
EPS = 1e-6
NEG_INF = -1e30

LANES = 128
BF16_SUBLANES = 16
TOK_BLOCK = NA_QROWS * GRID_W
NA_STEP_BLOCKS = 2
NA_STEP_ROWS = NA_STEP_BLOCKS * NA_QROWS
NA_KEY_ROWS = NA_STEP_ROWS + NA_MAX_ROWS
NA_KEY_BLOCKS = NA_KEY_ROWS * GRID_W // TOK_BLOCK
NA_KEYS = NA_KEY_BLOCKS * TOK_BLOCK
SW_HALF_KEYS = 3 * SW_BLOCK
MM_TILE = 1024
GATEUP_BN = 512
FF_MAIN = D_FF // GATEUP_BN * GATEUP_BN
FF_TAIL = D_FF - FF_MAIN
PRENORM_ROWS = 512
POST_ROWS = 256
VMEM_LIMIT = 56 * 1024 * 1024
VMEM_SLACK = 2 * 1024 * 1024
LOG2E = math.log2(math.e)

assert SW_BLOCK == TOK_BLOCK and DEPTH == 1 and FF_TAIL == 256


def _cparams(sem, vmem=VMEM_LIMIT):
    return pltpu.CompilerParams(dimension_semantics=sem, vmem_limit_bytes=vmem)


def _block_tables(seq_blocks):
    kb, pat, prev, nxt, swpat = [], [], [], [], []
    patterns = []
    start_blk = 0
    for nb in seq_blocks:
        rows = nb * NA_QROWS
        assert nb % NA_STEP_BLOCKS == 0 and rows >= NA_KEY_ROWS
        for st in range(nb // NA_STEP_BLOCKS):
            r0 = st * NA_STEP_ROWS
            ks = max(r0 - NA_MAX_ROWS // 2, 0)
            assert ks % NA_QROWS == 0
            w = tuple(int(np.clip(r0 + a - NA_MAX_ROWS // 2, 0, rows - NA_MAX_ROWS)) - ks
                      for a in range(NA_STEP_ROWS))
            assert all(0 <= wa and wa + NA_MAX_ROWS <= NA_KEY_ROWS - 1 for wa in w)
            key = (r0 - ks, w)
            if key not in patterns:
                patterns.append(key)
            kb.append(start_blk + ks // NA_QROWS)
            pat.append(patterns.index(key))
        for bl in range(nb):
            prev.append(start_blk + max(bl - 1, 0))
            nxt.append(start_blk + min(bl + 1, nb - 1))
            swpat.append(int(bl == 0) + 2 * int(bl == nb - 1))
        start_blk += nb
    i32 = lambda v: jnp.asarray(np.asarray(v, np.int32))
    return dict(kb=i32(kb), pat=i32(pat), prev=i32(prev), nxt=i32(nxt), swpat=i32(swpat)), patterns


def _rms(x, g):
    return x * lax.rsqrt(jnp.mean(x * x, axis=-1, keepdims=True) + EPS) * g


def _stream_rows(body, tb, d, streams, vecs, out, steps, out_block0):
    deep = pl.Buffered(3)
    in_specs = [pl.BlockSpec((tb, d), functools.partial(lambda i, b0: (i + b0, 0), b0=b0), pipeline_mode=deep)
                for _, b0 in streams]
    in_specs += [pl.BlockSpec((1, d), lambda i: (0, 0))] * len(vecs)
    out_specs = [pl.BlockSpec((tb, d), lambda i: (i + out_block0, 0))]
    pltpu.emit_pipeline(body, grid=(steps,), in_specs=in_specs, out_specs=out_specs)(
        *[r for r, _ in streams], *vecs, out)


def _stream_vmem(tb, d, in_bytes, out_bytes):
    return 3 * tb * d * sum(in_bytes) + 2 * tb * d * out_bytes + 2 * tb * d * 4 + VMEM_SLACK


_ANY = pl.BlockSpec(memory_space=pl.ANY)


def _prenorm1_kernel(x_ref, g_ref, o_ref):
    o_ref[...] = _rms(x_ref[...], g_ref[...]).astype(o_ref.dtype)


def _prenorm2(xp, xs, g, tb=PRENORM_ROWS):
    n_first, n_second = xp.shape[0] // tb, xs.shape[0] // tb
    d = xp.shape[1]

    def streamed(xp_hbm, xs_hbm, g_hbm, o_hbm):
        _stream_rows(_prenorm1_kernel, tb, d, [(xp_hbm, 0)], [g_hbm], o_hbm, n_first, 0)
        _stream_rows(_prenorm1_kernel, tb, d, [(xs_hbm, 0)], [g_hbm], o_hbm, n_second, n_first)

    return pl.pallas_call(
        streamed,
        in_specs=[_ANY] * 3,
        out_specs=_ANY,
        out_shape=jax.ShapeDtypeStruct((xp.shape[0] + xs.shape[0], d), jnp.bfloat16),
        compiler_params=pltpu.CompilerParams(vmem_limit_bytes=_stream_vmem(tb, d, (4,), 2)),
        name="prenorm",
    )(xp, xs, g)


def _prenorm1(x, g):
    n, d = x.shape
    return pl.pallas_call(
        _prenorm1_kernel,
        grid=(1,),
        in_specs=[pl.BlockSpec((n, d), lambda i: (0, 0)), pl.BlockSpec((1, d), lambda i: (0, 0))],
        out_specs=pl.BlockSpec((n, d), lambda i: (0, 0)),
        out_shape=jax.ShapeDtypeStruct((n, d), jnp.bfloat16),
        compiler_params=_cparams(("arbitrary",)),
        name="prenorm_meta",
    )(x, g)


def _mm_in_kernel(a_ref, b_ref, s_ref, o_ref):
    acc = jnp.dot(a_ref[...], b_ref[...].astype(jnp.bfloat16), preferred_element_type=jnp.float32)
    o_ref[...] = (acc * s_ref[...]).astype(o_ref.dtype)


def _mm_in(a, b, colscale, bm, bn=MM_TILE, name="mm_in", col_steps=None, col_of=lambda j: j):
    m, k = a.shape
    col_steps = b.shape[1] // bn if col_steps is None else col_steps
    return pl.pallas_call(
        _mm_in_kernel,
        grid=(m // bm, col_steps),
        in_specs=[
            pl.BlockSpec((bm, k), lambda i, j: (i, 0)),
            pl.BlockSpec((k, bn), lambda i, j: (0, col_of(j))),
            pl.BlockSpec((1, bn), lambda i, j: (0, col_of(j))),
        ],
        out_specs=pl.BlockSpec((bm, bn), lambda i, j: (i, j)),
        out_shape=jax.ShapeDtypeStruct((m, col_steps * bn), jnp.bfloat16),
        compiler_params=_cparams(("parallel", "parallel"),
                                 2 * bm * k * 2 + 2 * k * bn * 4 + k * bn * 2 + 2 * bm * bn * 2 + VMEM_SLACK),
        name=name,
    )(a, b, colscale)


def _mm_out_kernel(a1_ref, a2_ref, b1_ref, b2_ref, o_ref):
    acc = jnp.dot(a1_ref[...], b1_ref[...].astype(jnp.bfloat16), preferred_element_type=jnp.float32)
    acc = acc + jnp.dot(a2_ref[...], b2_ref[...].astype(jnp.bfloat16), preferred_element_type=jnp.float32)
    o_ref[...] = acc.astype(o_ref.dtype)


def _mm_out(a1, a2, b, bm=MM_TILE, bn=MM_TILE):
    m, k1 = a1.shape
    k2 = a2.shape[1]
    n = b.shape[1]
    assert k1 == k2 and b.shape[0] == k1 + k2
    vmem = 4 * bm * k1 * 2 + 4 * k1 * bn * 4 + 2 * k1 * bn * 2 + 2 * bm * bn * 2 + VMEM_SLACK
    return pl.pallas_call(
        _mm_out_kernel,
        grid=(m // bm, n // bn),
        in_specs=[
            pl.BlockSpec((bm, k1), lambda i, j: (i, 0)),
            pl.BlockSpec((bm, k2), lambda i, j: (i, 0)),
            pl.BlockSpec((k1, bn), lambda i, j: (0, j)),
            pl.BlockSpec((k2, bn), lambda i, j: (1, j)),
        ],
        out_specs=pl.BlockSpec((bm, bn), lambda i, j: (i, j)),
        out_shape=jax.ShapeDtypeStruct((m, n), jnp.bfloat16),
        compiler_params=_cparams(("parallel", "parallel"), vmem),
        name="mm_out",
    )(a1, a2, b, b)


def _mm_gateup_kernel(a_ref, bg_ref, bu_ref, o_ref):
    a = a_ref[...]
    g = jnp.dot(a, bg_ref[...].astype(jnp.bfloat16), preferred_element_type=jnp.float32)
    u = jnp.dot(a, bu_ref[...].astype(jnp.bfloat16), preferred_element_type=jnp.float32)
    o_ref[...] = (g / (1.0 + jnp.exp(-g)) * u).astype(o_ref.dtype)


def _mm_gateup(a, bg, bu, col0, ncols, bn, name, bm=MM_TILE):
    m, k = a.shape
    assert col0 % bn == 0 and ncols % bn == 0
    j0 = col0 // bn
    wspec = pl.BlockSpec((k, bn), lambda i, j: (0, j + j0))
    return pl.pallas_call(
        _mm_gateup_kernel,
        grid=(m // bm, ncols // bn),
        in_specs=[pl.BlockSpec((bm, k), lambda i, j: (i, 0)), wspec, wspec],
        out_specs=pl.BlockSpec((bm, bn), lambda i, j: (i, j)),
        out_shape=jax.ShapeDtypeStruct((m, ncols), jnp.bfloat16),
        compiler_params=_cparams(("parallel", "parallel"),
                                 2 * bm * k * 2 + 4 * k * bn * 4 + 2 * k * bn * 2 + 2 * bm * bn * 2 + VMEM_SLACK),
        name=name,
    )(a, bg, bu)


def _mm_down_kernel(a1_ref, a2_ref, b1_ref, b2_ref, o_ref, acc_ref):
    kk = pl.program_id(2)

    @pl.when(kk == 0)
    def _():
        acc_ref[...] = jnp.dot(a2_ref[...], b2_ref[...].astype(jnp.bfloat16), preferred_element_type=jnp.float32)

    acc_ref[...] += jnp.dot(a1_ref[...], b1_ref[...].astype(jnp.bfloat16), preferred_element_type=jnp.float32)

    @pl.when(kk == pl.num_programs(2) - 1)
    def _():
        o_ref[...] = acc_ref[...].astype(o_ref.dtype)


def _mm_down(a1, a2, b, bm=MM_TILE, bn=MM_TILE, ksteps=3):
    m, k1 = a1.shape
    k2 = a2.shape[1]
    n = b.shape[1]
    bk = k1 // ksteps
    assert b.shape[0] == k1 + k2 and k1 % k2 == 0 and k1 % ksteps == 0 and bk % LANES == 0
    vmem = (2 * bm * (bk + k2) * 2 + 2 * (bk + k2) * bn * 4 + (bk + k2) * bn * 2 + 2 * bm * bn * 2
            + bm * bn * 4 + VMEM_SLACK)
    return pl.pallas_call(
        _mm_down_kernel,
        grid=(m // bm, n // bn, ksteps),
        in_specs=[
            pl.BlockSpec((bm, bk), lambda i, j, kk: (i, kk)),
            pl.BlockSpec((bm, k2), lambda i, j, kk: (i, 0)),
            pl.BlockSpec((bk, bn), lambda i, j, kk: (kk, j)),
            pl.BlockSpec((k2, bn), lambda i, j, kk: (k1 // k2, j)),
        ],
        out_specs=pl.BlockSpec((bm, bn), lambda i, j, kk: (i, j)),
        out_shape=jax.ShapeDtypeStruct((m, n), jnp.bfloat16),
        scratch_shapes=[pltpu.VMEM((bm, bn), jnp.float32)],
        compiler_params=_cparams(("parallel", "parallel", "arbitrary"), vmem),
        name="mm_down",
    )(a1, a2, b, b)


def _post1_kernel(x_ref, mix_ref, gpost_ref, gpre_ref, h_ref):
    x1 = x_ref[...] + _rms(mix_ref[...].astype(jnp.float32), gpost_ref[...])
    h_ref[...] = _rms(x1, gpre_ref[...]).astype(h_ref.dtype)


def _post1(xp, xs, mix, gpost, gpre, tb=POST_ROWS):
    n_first, n_second = xp.shape[0] // tb, xs.shape[0] // tb
    d = xp.shape[1]

    def streamed(xp_hbm, xs_hbm, mix_hbm, gpost_hbm, gpre_hbm, h_hbm):
        vecs = [gpost_hbm, gpre_hbm]
        _stream_rows(_post1_kernel, tb, d, [(xp_hbm, 0), (mix_hbm, 0)], vecs, h_hbm, n_first, 0)
        _stream_rows(_post1_kernel, tb, d, [(xs_hbm, 0), (mix_hbm, n_first)], vecs, h_hbm, n_second, n_first)

    return pl.pallas_call(
        streamed,
        in_specs=[_ANY] * 5,
        out_specs=_ANY,
        out_shape=jax.ShapeDtypeStruct((mix.shape[0], d), jnp.bfloat16),
        compiler_params=pltpu.CompilerParams(vmem_limit_bytes=_stream_vmem(tb, d, (4, 2), 2)),
        name="post_mix",
    )(xp, xs, mix, gpost, gpre)


def _post2_kernel(x_ref, mix_ref, f_ref, gmix_ref, gffn_ref, o_ref):
    x1 = x_ref[...] + _rms(mix_ref[...].astype(jnp.float32), gmix_ref[...])
    o_ref[...] = x1 + _rms(f_ref[...].astype(jnp.float32), gffn_ref[...])


def _post2(x, mix, f, gmix, gffn, row0, tb=POST_ROWS):
    nrows, d = x.shape
    b0 = row0 // tb

    def streamed(x_hbm, mix_hbm, f_hbm, gmix_hbm, gffn_hbm, o_hbm):
        _stream_rows(_post2_kernel, tb, d, [(x_hbm, 0), (mix_hbm, b0), (f_hbm, b0)], [gmix_hbm, gffn_hbm],
                     o_hbm, nrows // tb, 0)

    return pl.pallas_call(
        streamed,
        in_specs=[_ANY] * 5,
        out_specs=_ANY,
        out_shape=jax.ShapeDtypeStruct((nrows, d), jnp.float32),
        compiler_params=pltpu.CompilerParams(vmem_limit_bytes=_stream_vmem(tb, d, (4, 2, 2), 4)),
        name="post_ffn",
    )(x, mix, f, gmix, gffn)


def _na_bias_kernel(rpb_ref, o_ref, *, patterns):
    h = pl.program_id(0)
    n_ic = 2 * NA_COLS - 1
    n_ir = 2 * NA_MAX_ROWS - 1
    shape = (GRID_W, LANES)
    qc = lax.broadcasted_iota(jnp.int32, shape, 0)
    lane = lax.broadcasted_iota(jnp.int32, shape, 1)
    kc = lane & (GRID_W - 1)
    delta = kc - qc + (NA_COLS - 1)
    cq = jnp.clip(qc - NA_COLS // 2, 0, GRID_W - NA_COLS)
    col_ok = (kc >= cq) & (kc < cq + NA_COLS)
    left = lane < GRID_W
    neg = jnp.full(shape, NEG_INF, jnp.float32)

    w = []
    for e in range(n_ir):
        acc = jnp.zeros(shape, jnp.float32)
        for d in range(n_ic):
            acc = jnp.where(delta == d, rpb_ref[h * (n_ir * n_ic) + e * n_ic + d], acc)
        w.append(acc * LOG2E)
    zero = jnp.zeros(shape, jnp.float32)

    def tile(dr):
        e = dr + NA_MAX_ROWS - 1
        return w[e] if 0 <= e < n_ir else zero

    for p, (rel, win) in enumerate(patterns):
        for a in range(NA_STEP_ROWS):
            for c in range(NA_KEY_BLOCKS):
                b0 = 2 * c
                dr = b0 - a - rel
                ok0 = win[a] <= b0 < win[a] + NA_MAX_ROWS
                ok1 = win[a] <= b0 + 1 < win[a] + NA_MAX_ROWS
                pair = jnp.where(left, tile(dr), tile(dr + 1))
                if ok0 and ok1:
                    ok = col_ok
                elif ok0:
                    ok = col_ok & left
                elif ok1:
                    ok = col_ok & jnp.logical_not(left)
                else:
                    ok = None
                val = neg if ok is None else jnp.where(ok, pair, neg)
                if c == NA_KEY_BLOCKS - 1:
                    assert not ok1
                    val = jnp.where(lane >= LANES - N_META, 0.0, val)
                o_ref[p, 0, a * GRID_W:(a + 1) * GRID_W, c * LANES:(c + 1) * LANES] = val


def _na_bias(rpb_flat, patterns):
    npat = len(patterns)
    q = NA_STEP_BLOCKS * TOK_BLOCK
    return pl.pallas_call(
        functools.partial(_na_bias_kernel, patterns=tuple(patterns)),
        grid=(N_NA_HEADS,),
        in_specs=[pl.BlockSpec(memory_space=pltpu.SMEM)],
        out_specs=pl.BlockSpec((npat, 1, q, NA_KEYS), lambda h: (0, h, 0, 0)),
        out_shape=jax.ShapeDtypeStruct((npat, N_NA_HEADS, q, NA_KEYS), jnp.float32),
        compiler_params=_cparams(("parallel",)),
        name="na_bias",
    )(rpb_flat)


_NT = (((1,), (1,)), ((), ()))


def _group_norm_store(oacc_ref, ss, g_ref, o_ref):
    inv = lax.rsqrt(ss * (1.0 / oacc_ref.shape[1]) + EPS)
    o_ref[...] = (oacc_ref[...] * inv * g_ref[...]).astype(o_ref.dtype)


def _na_kernel(kb_ref, pat_ref, q_ref, *rest):
    k_refs = rest[:NA_KEY_BLOCKS]
    v_refs = rest[NA_KEY_BLOCKS:2 * NA_KEY_BLOCKS]
    km_ref, vm_ref, bias_ref, g_ref, o_ref, oacc_ref, sa_ref, sb_ref = rest[2 * NA_KEY_BLOCKS:]
    nq = q_ref.shape[0]
    real = TOK_BLOCK - BF16_SUBLANES

    def with_meta(refs, m_ref, sl):
        return jnp.concatenate([r[:, sl] for r in refs[:-1]] + [refs[-1][:real, sl], m_ref[real:, sl]], axis=0)

    def logits(h, s_ref):
        sl = slice(h * HEAD_DIM, (h + 1) * HEAD_DIM)
        k = with_meta(k_refs, km_ref, sl)
        s_ref[...] = lax.dot_general(q_ref[:, sl], k, _NT, preferred_element_type=jnp.float32) + bias_ref[0, h]

    def softmax_pv(h, s_ref, ss):
        sl = slice(h * HEAD_DIM, (h + 1) * HEAD_DIM)
        s = s_ref[...]
        m = jnp.max(s, axis=-1, keepdims=True)
        p = jnp.exp2(s - m)
        v = with_meta(v_refs, vm_ref, sl)
        v1 = jnp.concatenate([v, jnp.ones_like(v)], axis=1)
        o2 = jnp.dot(p.astype(v.dtype), v1, preferred_element_type=jnp.float32)
        o = o2[:, :HEAD_DIM] / o2[:, HEAD_DIM:HEAD_DIM + 1]
        oacc_ref[:, sl] = o
        return ss + jnp.sum(o * o, axis=-1, keepdims=True)

    bufs = (sa_ref, sb_ref)
    ss = jnp.zeros((nq, 1), jnp.float32)
    logits(0, bufs[0])
    for h in range(N_NA_HEADS):
        if h + 1 < N_NA_HEADS:
            logits(h + 1, bufs[(h + 1) % 2])
        ss = softmax_pv(h, bufs[h % 2], ss)
    _group_norm_store(oacc_ref, ss, g_ref, o_ref)


def _na_attention(proj, projm, bias, g, tables):
    n = proj.shape[0]
    nblk = n // TOK_BLOCK
    nq = NA_STEP_BLOCKS * TOK_BLOCK
    qblk = (nq, D_NA)
    kblk = (TOK_BLOCK, D_NA)

    def kv_spec(col, j):
        return pl.BlockSpec(kblk, lambda i, kb, pat: (jnp.minimum(kb[i] + j, nblk - 1), col))

    in_specs = [pl.BlockSpec(qblk, lambda i, kb, pat: (i, 0))]
    in_specs += [kv_spec(1, j) for j in range(NA_KEY_BLOCKS)]
    in_specs += [kv_spec(2, j) for j in range(NA_KEY_BLOCKS)]
    in_specs += [
        pl.BlockSpec(kblk, lambda i, kb, pat: (0, 0)),
        pl.BlockSpec(kblk, lambda i, kb, pat: (0, 1)),
        pl.BlockSpec((1, N_NA_HEADS, nq, NA_KEYS), lambda i, kb, pat: (pat[i], 0, 0, 0)),
        pl.BlockSpec((1, D_NA), lambda i, kb, pat: (0, 0)),
    ]
    return pl.pallas_call(
        _na_kernel,
        grid_spec=pltpu.PrefetchScalarGridSpec(
            num_scalar_prefetch=2,
            grid=(n // nq,),
            in_specs=in_specs,
            out_specs=pl.BlockSpec(qblk, lambda i, kb, pat: (i, 0)),
            scratch_shapes=[pltpu.VMEM(qblk, jnp.float32),
                            pltpu.VMEM((nq, NA_KEYS), jnp.float32),
                            pltpu.VMEM((nq, NA_KEYS), jnp.float32)],
        ),
        out_shape=jax.ShapeDtypeStruct((n, D_NA), jnp.bfloat16),
        compiler_params=_cparams(("arbitrary",)),
        name="na_attention",
    )(tables["kb"], tables["pat"], proj, *([proj] * (2 * NA_KEY_BLOCKS)), projm, projm, bias, g)


def _swa_bias_kernel(slope_ref, o_ref):
    slope = slope_ref[pl.program_id(0)]
    shape = (SW_BLOCK, SW_HALF_KEYS)
    half = SW_BLOCK // 2
    r = lax.broadcasted_iota(jnp.int32, shape, 0)
    c = lax.broadcasted_iota(jnp.int32, shape, 1)
    lo = r < half
    hi = jnp.logical_not(lo)
    meta0_lo = 2 * SW_BLOCK + half
    meta0_hi = half
    is_real = (lo & (c < meta0_lo)) | (hi & ((c < half) | (c >= SW_BLOCK)))
    is_meta = (lo & (c >= meta0_lo) & (c < meta0_lo + N_META)) | (hi & (c >= meta0_hi) & (c < meta0_hi + N_META))
    kc = jnp.where(hi & (c < half), c + half, c)
    j = jnp.where(lo, c - meta0_lo, c - meta0_hi)
    rel = kc - r
    band = (rel >= 0) & (rel <= 2 * SW_WINDOW)
    for p in range(4):
        first, last = p & 1, p >> 1
        c_lo = SW_BLOCK if first else 0
        c_hi = 2 * SW_BLOCK if last else 3 * SW_BLOCK
        valid = (is_real & band & (kc >= c_lo) & (kc < c_hi)) | is_meta
        mdist = jnp.minimum(N_META + r - j, SW_WINDOW) if first else jnp.full(shape, SW_WINDOW, jnp.int32)
        dist = jnp.where(is_real, jnp.abs(rel - SW_BLOCK), mdist).astype(jnp.float32)
        o_ref[p, 0] = jnp.where(valid, -slope * dist, NEG_INF)


def _swa_bias(slopes):
    return pl.pallas_call(
        _swa_bias_kernel,
        grid=(N_SW_HEADS,),
        in_specs=[pl.BlockSpec(memory_space=pltpu.SMEM)],
        out_specs=pl.BlockSpec((4, 1, SW_BLOCK, SW_HALF_KEYS), lambda h: (0, h, 0, 0)),
        out_shape=jax.ShapeDtypeStruct((4, N_SW_HEADS, SW_BLOCK, SW_HALF_KEYS), jnp.float32),
        compiler_params=_cparams(("parallel",)),
        name="swa_bias",
    )(slopes)


def _swa_kernel(prev_ref, nxt_ref, pat_ref, sink_ref, q_ref, kp_ref, kc_ref, kn_ref,
                vp_ref, vc_ref, vn_ref, km_ref, vm_ref, bias_ref, g_ref, o_ref, oacc_ref, sa_ref, sb_ref):
    half = SW_BLOCK // 2
    nrow = SW_GROUP * half

    def group_heads(kh):
        return [kh * SW_GROUP + gi for gi in range(SW_GROUP)]

    def keys(p_ref, c_ref, n_ref, m_ref, ksl):
        lo = jnp.concatenate([p_ref[:, ksl], c_ref[:, ksl], n_ref[:half, ksl], m_ref[:half, ksl]], axis=0)
        hi = jnp.concatenate([p_ref[half:, ksl], m_ref[:half, ksl], c_ref[:, ksl], n_ref[:, ksl]], axis=0)
        return lo, hi

    def logits(kh, s_ref):
        ksl = slice(kh * HEAD_DIM, (kh + 1) * HEAD_DIM)
        k_lo, k_hi = keys(kp_ref, kc_ref, kn_ref, km_ref, ksl)
        for part, (k, r0) in enumerate(((k_lo, 0), (k_hi, half))):
            q = jnp.concatenate([q_ref[r0:r0 + half, h * HEAD_DIM:(h + 1) * HEAD_DIM] for h in group_heads(kh)], axis=0)
            s_all = lax.dot_general(q, k, _NT, preferred_element_type=jnp.float32)
            for gi, h in enumerate(group_heads(kh)):
                rows = slice(gi * half, (gi + 1) * half)
                s_ref[part * nrow + gi * half:part * nrow + (gi + 1) * half] = s_all[rows] + bias_ref[0, h, r0:r0 + half]

    def softmax_pv(kh, s_ref, ss):
        ksl = slice(kh * HEAD_DIM, (kh + 1) * HEAD_DIM)
        v_lo, v_hi = keys(vp_ref, vc_ref, vn_ref, vm_ref, ksl)
        new_ss = []
        for part, (v, r0) in enumerate(((v_lo, 0), (v_hi, half))):
            ps, ls = [], []
            for gi, h in enumerate(group_heads(kh)):
                s = s_ref[part * nrow + gi * half:part * nrow + (gi + 1) * half]
                sink = sink_ref[h] * LOG2E
                m = jnp.maximum(jnp.max(s, axis=-1, keepdims=True), sink)
                p = jnp.exp2(s - m)
                ls.append(jnp.sum(p, axis=-1, keepdims=True) + jnp.exp2(sink - m))
                ps.append(p.astype(v.dtype))
            o_all = jnp.dot(jnp.concatenate(ps, axis=0), v, preferred_element_type=jnp.float32)
            acc = ss[part]
            for gi, h in enumerate(group_heads(kh)):
                o = o_all[gi * half:(gi + 1) * half] / ls[gi]
                acc = acc + jnp.sum(o * o, axis=-1, keepdims=True)
                oacc_ref[r0:r0 + half, h * HEAD_DIM:(h + 1) * HEAD_DIM] = o
            new_ss.append(acc)
        return new_ss

    bufs = (sa_ref, sb_ref)
    ss = [jnp.zeros((half, 1), jnp.float32), jnp.zeros((half, 1), jnp.float32)]
    logits(0, bufs[0])
    for kh in range(N_SW_KV_HEADS):
        if kh + 1 < N_SW_KV_HEADS:
            logits(kh + 1, bufs[(kh + 1) % 2])
        ss = softmax_pv(kh, bufs[kh % 2], ss)
    _group_norm_store(oacc_ref, jnp.concatenate(ss, axis=0), g_ref, o_ref)


def _swa_attention(proj, projm, sink, bias, g, tables):
    n = proj.shape[0]
    nblk = n // SW_BLOCK
    qcol = 3 * D_NA // D_SW
    kcol = (3 * D_NA + D_SW) // D_SW_KV
    vcol = kcol + 1
    qblk = (SW_BLOCK, D_SW)
    kvblk = (SW_BLOCK, D_SW_KV)

    def kv(col, which):
        if which == "prev":
            return pl.BlockSpec(kvblk, lambda i, pv, nx, pt: (pv[i], col))
        if which == "cur":
            return pl.BlockSpec(kvblk, lambda i, pv, nx, pt: (i, col))
        if which == "next":
            return pl.BlockSpec(kvblk, lambda i, pv, nx, pt: (nx[i], col))
        return pl.BlockSpec(kvblk, lambda i, pv, nx, pt: (0, col))

    in_specs = [pl.BlockSpec(memory_space=pltpu.SMEM),
                pl.BlockSpec(qblk, lambda i, pv, nx, pt: (i, qcol))]
    in_specs += [kv(kcol, w) for w in ("prev", "cur", "next")]
    in_specs += [kv(vcol, w) for w in ("prev", "cur", "next")]
    mcol = 2 * D_NA // D_SW_KV
    in_specs += [kv(mcol, "meta"), kv(mcol + 1, "meta"),
                 pl.BlockSpec((1, N_SW_HEADS, SW_BLOCK, SW_HALF_KEYS), lambda i, pv, nx, pt: (pt[i], 0, 0, 0)),
                 pl.BlockSpec((1, D_SW), lambda i, pv, nx, pt: (0, 0))]
    return pl.pallas_call(
        _swa_kernel,
        grid_spec=pltpu.PrefetchScalarGridSpec(
            num_scalar_prefetch=3,
            grid=(nblk,),
            in_specs=in_specs,
            out_specs=pl.BlockSpec(qblk, lambda i, pv, nx, pt: (i, 0)),
            scratch_shapes=[pltpu.VMEM(qblk, jnp.float32),
                            pltpu.VMEM((SW_GROUP * SW_BLOCK, SW_HALF_KEYS), jnp.float32),
                            pltpu.VMEM((SW_GROUP * SW_BLOCK, SW_HALF_KEYS), jnp.float32)],
        ),
        out_shape=jax.ShapeDtypeStruct((n, D_SW), jnp.bfloat16),
        compiler_params=_cparams(("arbitrary",)),
        name="swa_attention",
    )(tables["prev"], tables["nxt"], tables["swpat"], sink, proj,
      proj, proj, proj, proj, proj, proj, projm, projm, bias, g)


def kernel(x_prompt, x_sample, meta_tokens, w_in, w_out, rpb, sink, g_na_out, g_sw_out,
           g_pre_mix, g_post_mix, g_pre_ffn, g_post_ffn, w_gate, w_up, w_down):
    xp = x_prompt.reshape(-1, D_MODEL)
    xs = x_sample.reshape(-1, D_MODEL)
    n_p = xp.shape[0]
    seq_blocks = [x_prompt.shape[1] // TOK_BLOCK] * x_prompt.shape[0]
    seq_blocks += [x_sample.shape[1] // TOK_BLOCK] * x_sample.shape[0]
    tables, patterns = _block_tables(seq_blocks)
    row = lambda v: v.reshape(1, -1)

    scale = HEAD_DIM ** -0.5 * LOG2E
    colscale = np.ones((1, D_IN), np.float32)
    colscale[:, :D_NA] = scale
    colscale[:, 3 * D_NA:3 * D_NA + D_SW] = scale
    colscale = jnp.asarray(colscale)

    h = _prenorm2(xp, xs, row(g_pre_mix[0]))
    proj = _mm_in(h, w_in[0], colscale, bm=MM_TILE)
    gap = jnp.zeros((TOK_BLOCK - 2 * N_META, D_MODEL), meta_tokens.dtype)
    meta_blk = jnp.concatenate([meta_tokens, gap, meta_tokens], axis=0)
    hm = _prenorm1(meta_blk, row(g_pre_mix[0]))
    bn = 2 * D_SW_KV
    n_kv, kv0, sw_kv = 2 * D_NA // bn, D_NA // bn, (3 * D_NA + D_SW) // bn
    projm = _mm_in(hm, w_in[0], colscale, bm=TOK_BLOCK, bn=bn, name="mm_in_meta", col_steps=n_kv + 1,
                   col_of=lambda j: jnp.where(j < n_kv, j + kv0, sw_kv))

    bias = _na_bias(rpb[0].reshape(-1), patterns)
    o_na = _na_attention(proj, projm, bias, row(g_na_out[0]), tables)
    slopes = np.asarray([2.0 ** (-8.0 * (i + 1) / N_SW_HEADS) * LOG2E for i in range(N_SW_HEADS)], np.float32)
    o_sw = _swa_attention(proj, projm, sink[0], _swa_bias(jnp.asarray(slopes)), row(g_sw_out[0]), tables)

    mix = _mm_out(o_na, o_sw, w_out[0])
    h2 = _post1(xp, xs, mix, row(g_post_mix[0]), row(g_pre_ffn[0]))

    act_main = _mm_gateup(h2, w_gate[0], w_up[0], 0, FF_MAIN, GATEUP_BN, "mm_gateup")
    act_tail = _mm_gateup(h2, w_gate[0], w_up[0], FF_MAIN, FF_TAIL, FF_TAIL, "mm_gateup_tail")
    f = _mm_down(act_main, act_tail, w_down[0])
    y_p = _post2(xp, mix, f, row(g_post_mix[0]), row(g_post_ffn[0]), 0)
    y_s = _post2(xs, mix, f, row(g_post_mix[0]), row(g_post_ffn[0]), n_p)
    return (y_p.reshape(x_prompt.shape), y_s.reshape(x_sample.shape))
```
